```python
import jax, jax.numpy as jnp
from jax import lax
import numpy as np

D_MODEL = 1024
BATCH = 4
SEQ = 8192
DEPTH = 2
DEC_BATCH = 128
DEC_SEQ = 4
PAST_LEN = 16384
PAGE_SIZE = 128

HEAD_DIM = 64
N_HEADS = D_MODEL // HEAD_DIM
KV_HEADS_A = 4
KV_HEADS_B = 4
GROUP_A = N_HEADS // KV_HEADS_A
GROUP_B = N_HEADS // KV_HEADS_B
MOBA_BLOCK = 256
MOBA_TOPK = 3
Q_CHUNK = 128
WINDOW = 128
ROPE_THETA = 10000.0
N_EXPERTS = 16
N_GROUPS = 4
EXPERTS_PER_GROUP = N_EXPERTS // N_GROUPS
TOP_K = 2
D_EXPERT = D_MODEL // 2
PLE_DIM = 256
N_A_LAYERS = DEPTH // 2
N_B_LAYERS = DEPTH - N_A_LAYERS
DEEPNORM_ALPHA = (2.0 * DEPTH) ** 0.25
DEEPNORM_BETA = (8.0 * DEPTH) ** -0.25
LN_EPS = 1e-5
ATTN_SCALE = HEAD_DIM ** -0.5

kernel_name = 'yoco_moba_swa_sink_grouped_moe_step'


def layer_norm(x, g, b):
    xf = x.astype(jnp.float32)
    mu = jnp.mean(xf, axis=-1, keepdims=True)
    var = jnp.mean(jnp.square(xf - mu), axis=-1, keepdims=True)
    return ((xf - mu) * lax.rsqrt(var + LN_EPS) * g.astype(jnp.float32) + b.astype(jnp.float32)).astype(x.dtype)


def rope(x, pos):
    half = HEAD_DIM // 2
    inv_freq = ROPE_THETA ** (-jnp.arange(half, dtype=jnp.float32) / half)
    ang = pos.astype(jnp.float32)[:, None] * inv_freq[None, :]
    cos = jnp.cos(ang)[:, None, :]
    sin = jnp.sin(ang)[:, None, :]
    xf = x.astype(jnp.float32)
    x1, x2 = xf[..., :half], xf[..., half:]
    return jnp.concatenate([x1 * cos - x2 * sin, x2 * cos + x1 * sin], axis=-1).astype(x.dtype)


def sink_softmax(logits, sink):
    m = jnp.maximum(jnp.max(logits, axis=-1, keepdims=True), sink)
    e = jnp.exp(logits - m)
    return e / (jnp.sum(e, axis=-1, keepdims=True) + jnp.exp(sink - m))


def moba_prompt(q, k, v):
    b, s = q.shape[0], q.shape[1]
    n_blk = -(-s // MOBA_BLOCK)
    pad = n_blk * MOBA_BLOCK - s
    k_pad = jnp.pad(k, ((0, 0), (0, pad), (0, 0), (0, 0)))
    v_pad = jnp.pad(v, ((0, 0), (0, pad), (0, 0), (0, 0)))
    k_blk = k_pad.reshape(b, n_blk, MOBA_BLOCK, KV_HEADS_A, HEAD_DIM)
    v_blk = v_pad.reshape(b, n_blk, MOBA_BLOCK, KV_HEADS_A, HEAD_DIM)
    k_mean = jnp.mean(k_blk.astype(jnp.float32), axis=2)
    n_sel = min(MOBA_TOPK, (s - 1) // MOBA_BLOCK)
    b_idx = jnp.arange(b)[:, None, None, None, None]
    h_idx = jnp.arange(KV_HEADS_A)[None, None, :, None, None]
    blk_ids = jnp.arange(n_blk)

    def chunk(c):
        start = c * Q_CHUNK
        qc = lax.dynamic_slice_in_dim(q, start, Q_CHUNK, axis=1)
        qc = qc.reshape(b, Q_CHUNK, KV_HEADS_A, GROUP_A, HEAD_DIM).astype(jnp.float32)
        qpos = start + jnp.arange(Q_CHUNK)
        blk = start // MOBA_BLOCK
        bstart = blk * MOBA_BLOCK
        k_own = lax.dynamic_slice_in_dim(k_pad, bstart, MOBA_BLOCK, axis=1)
        v_own = lax.dynamic_slice_in_dim(v_pad, bstart, MOBA_BLOCK, axis=1)
        kpos = bstart + jnp.arange(MOBA_BLOCK)
        s_own = jnp.einsum('bqkgd,bpkd->bqkgp', qc, k_own.astype(jnp.float32)) * ATTN_SCALE
        s_own = jnp.where((kpos[None, :] <= qpos[:, None])[None, :, None, None, :], s_own, -jnp.inf)
        if n_sel == 0:
            p_own = jax.nn.softmax(s_own, axis=-1).astype(v.dtype)
            return jnp.einsum('bqkgp,bpkd->bqkgd', p_own, v_own)
        gate = jnp.einsum('bqkgd,bnkd->bqkgn', qc, k_mean)
        gate = jnp.where(blk_ids < blk, gate, -jnp.inf)
        _, sel = lax.top_k(gate, n_sel)
        k_sel = k_blk[b_idx, sel, :, h_idx]
        v_sel = v_blk[b_idx, sel, :, h_idx].reshape(b, Q_CHUNK, KV_HEADS_A, GROUP_A, n_sel * MOBA_BLOCK, HEAD_DIM)
        s_sel = jnp.einsum('bqkgd,bqkgnpd->bqkgnp', qc, k_sel.astype(jnp.float32)) * ATTN_SCALE
        s_sel = jnp.where((sel < blk)[..., None], s_sel, -jnp.inf)
        s_sel = s_sel.reshape(b, Q_CHUNK, KV_HEADS_A, GROUP_A, n_sel * MOBA_BLOCK)
        p = jax.nn.softmax(jnp.concatenate([s_sel, s_own], axis=-1), axis=-1).astype(v.dtype)
        n_past = n_sel * MOBA_BLOCK
        return (jnp.einsum('bqkgp,bqkgpd->bqkgd', p[..., :n_past], v_sel)
                + jnp.einsum('bqkgp,bpkd->bqkgd', p[..., n_past:], v_own))

    out = lax.map(chunk, jnp.arange(s // Q_CHUNK))
    return jnp.moveaxis(out, 0, 1).reshape(b, s, N_HEADS * HEAD_DIM)


def moba_sample(layer, q, k_new, v_new, k_pool, v_pool, page_table):
    db, t = q.shape[0], q.shape[1]
    ppb = MOBA_BLOCK // PAGE_SIZE
    n_pages = PAST_LEN // PAGE_SIZE
    n_full = PAST_LEN // MOBA_BLOCK
    full_pages = n_full * ppb
    n_tail = (n_pages - full_pages) * PAGE_SIZE
    n_sel = min(MOBA_TOPK, n_full)
    l1 = jnp.full((1,), layer, dtype=jnp.int32)
    l5 = jnp.full((1, 1, 1, 1, 1), layer, dtype=jnp.int32)
    h_idx = jnp.arange(KV_HEADS_A)[None, :, None, None]
    causal_new = jnp.arange(t)[None, :] <= jnp.arange(t)[:, None]
    own_mask = jnp.concatenate([jnp.ones((t, n_tail), dtype=bool), causal_new], axis=1)

    def one(args):
        qs, ks, vs, pt = args
        qg = qs.reshape(t, KV_HEADS_A, GROUP_A, HEAD_DIM).astype(jnp.float32)
        k_tail = k_pool[l1, pt[full_pages:]].reshape(n_tail, KV_HEADS_A, HEAD_DIM)
        v_tail = v_pool[l1, pt[full_pages:]].reshape(n_tail, KV_HEADS_A, HEAD_DIM)
        k_own = jnp.concatenate([k_tail, ks], axis=0)
        v_own = jnp.concatenate([v_tail, vs], axis=0)
        s_own = jnp.einsum('tkgd,pkd->tkgp', qg, k_own.astype(jnp.float32)) * ATTN_SCALE
        s_own = jnp.where(own_mask[:, None, None, :], s_own, -jnp.inf)
        if n_sel == 0:
            p_own = jax.nn.softmax(s_own, axis=-1).astype(vs.dtype)
            return jnp.einsum('tkgp,pkd->tkgd', p_own, v_own)
        k_full = k_pool[l1, pt[:full_pages]].reshape(n_full, MOBA_BLOCK, KV_HEADS_A, HEAD_DIM)
        k_mean = jnp.mean(k_full.astype(jnp.float32), axis=1)
        gate = jnp.einsum('tkgd,nkd->tkgn', qg, k_mean)
        _, sel = lax.top_k(gate, n_sel)
        k_sel = k_full[sel, :, h_idx]
        phys = pt[:full_pages].reshape(n_full, ppb)[sel]
        v_sel = v_pool[l5, phys, :, h_idx[..., None]].reshape(t, KV_HEADS_A, GROUP_A, n_sel * MOBA_BLOCK, HEAD_DIM)
        s_sel = jnp.einsum('tkgd,tkgnpd->tkgnp', qg, k_sel.astype(jnp.float32)) * ATTN_SCALE
        s_sel = s_sel.reshape(t, KV_HEADS_A, GROUP_A, n_sel * MOBA_BLOCK)
        p = jax.nn.softmax(jnp.concatenate([s_sel, s_own], axis=-1), axis=-1).astype(vs.dtype)
        n_past = n_sel * MOBA_BLOCK
        return (jnp.einsum('tkgp,tkgpd->tkgd', p[..., :n_past], v_sel)
                + jnp.einsum('tkgp,pkd->tkgd', p[..., n_past:], v_own))

    out = lax.map(one, (q, k_new, v_new, page_table))
    return out.reshape(db, t, N_HEADS * HEAD_DIM)


def swa_prompt(q, k, v, sinks):
    b, s = q.shape[0], q.shape[1]
    nb = s // WINDOW
    qb = q.reshape(b, nb, WINDOW, KV_HEADS_B, GROUP_B, HEAD_DIM).astype(jnp.float32)
    kb = k.reshape(b, nb, WINDOW, KV_HEADS_B, HEAD_DIM)
    vb = v.reshape(b, nb, WINDOW, KV_HEADS_B, HEAD_DIM)
    shift = ((0, 0), (1, 0), (0, 0), (0, 0), (0, 0))
    k_band = jnp.concatenate([jnp.pad(kb, shift)[:, :-1], kb], axis=2)
    v_band = jnp.concatenate([jnp.pad(vb, shift)[:, :-1], vb], axis=2)
    qrel = WINDOW + jnp.arange(WINDOW)
    krel = jnp.arange(2 * WINDOW)
    diff = qrel[:, None] - krel[None, :]
    band = (diff >= 0) & (diff < WINDOW)
    has_prev = (jnp.arange(nb) > 0)[:, None, None] | (krel >= WINDOW)[None, None, :]
    mask = band[None] & has_prev
    logits = jnp.einsum('bnqkgd,bnpkd->bnqkgp', qb, k_band.astype(jnp.float32)) * ATTN_SCALE
    logits = jnp.where(mask[None, :, :, None, None, :], logits, -jnp.inf)
    p = sink_softmax(logits, sinks.astype(jnp.float32).reshape(KV_HEADS_B, GROUP_B, 1))
    out = jnp.einsum('bnqkgp,bnpkd->bnqkgd', p.astype(v.dtype), v_band)
    return out.reshape(b, s, N_HEADS * HEAD_DIM)


def swa_sample(q, k_new, v_new, k_buf, v_buf, sinks):
    db, t = q.shape[0], q.shape[1]
    wb = k_buf.shape[1]
    k_all = jnp.concatenate([k_buf, k_new], axis=1)
    v_all = jnp.concatenate([v_buf, v_new], axis=1)
    kpos = jnp.concatenate([PAST_LEN - wb + jnp.arange(wb), PAST_LEN + jnp.arange(t)])
    qpos = PAST_LEN + jnp.arange(t)
    diff = qpos[:, None] - kpos[None, :]
    mask = (diff >= 0) & (diff < WINDOW)
    qg = q.reshape(db, t, KV_HEADS_B, GROUP_B, HEAD_DIM).astype(jnp.float32)
    logits = jnp.einsum('btkgd,bpkd->btkgp', qg, k_all.astype(jnp.float32)) * ATTN_SCALE
    logits = jnp.where(mask[None, :, None, None, :], logits, -jnp.inf)
    p = sink_softmax(logits, sinks.astype(jnp.float32).reshape(KV_HEADS_B, GROUP_B, 1))
    out = jnp.einsum('btkgp,bpkd->btkgd', p.astype(v_new.dtype), v_all)
    return out.reshape(db, t, N_HEADS * HEAD_DIM)


def grouped_moe(x, w_router, b_router, w_gate, w_up, w_down):
    lead = x.shape[:-1]
    xt = x.reshape(-1, D_MODEL)
    aff = jax.nn.sigmoid(jnp.einsum('td,de->te', xt.astype(jnp.float32), w_router.astype(jnp.float32)))
    biased = aff + b_router.astype(jnp.float32)
    grp_top = lax.top_k(biased.reshape(-1, N_GROUPS, EXPERTS_PER_GROUP), TOP_K)[0]
    best_group = jnp.argmax(jnp.sum(grp_top, axis=-1), axis=-1)
    in_group = (jnp.arange(N_EXPERTS) // EXPERTS_PER_GROUP)[None, :] == best_group[:, None]
    _, idx = lax.top_k(jnp.where(in_group, biased, -jnp.inf), TOP_K)
    w_sel = jnp.take_along_axis(aff, idx, axis=-1)
    w_sel = w_sel / jnp.sum(w_sel, axis=-1, keepdims=True)
    gates = jnp.sum(jax.nn.one_hot(idx, N_EXPERTS, dtype=jnp.float32) * w_sel[..., None], axis=1).astype(x.dtype)
    y = jnp.zeros_like(xt)
    for e in range(N_EXPERTS):
        h = jax.nn.silu(xt @ w_gate[e]) * (xt @ w_up[e])
        y = y + gates[:, e:e + 1] * (h @ w_down[e])
    return y.reshape(*lead, D_MODEL)


def trunk(x, p, pos, attend_a, attend_b, w_qkv_a, w_o_a, w_kv_s, w_q_b, w_o_b, ln_g, ln_b,
          w_router, b_router, w_exp_gate, w_exp_up, w_exp_down, w_ple, w_ple_gate):
    lead = x.shape[:2]
    hq = N_HEADS * HEAD_DIM
    hkv_a = KV_HEADS_A * HEAD_DIM
    hkv_b = KV_HEADS_B * HEAD_DIM
    k_rows, v_rows = [], []
    k_s, v_s = None, None
    for i in range(DEPTH):
        if i < N_A_LAYERS:
            qkv = x @ w_qkv_a[i]
            q, k, v = jnp.split(qkv, [hq, hq + hkv_a], axis=-1)
            q = rope(q.reshape(*lead, N_HEADS, HEAD_DIM), pos)
            k = rope(k.reshape(*lead, KV_HEADS_A, HEAD_DIM), pos)
            v = v.reshape(*lead, KV_HEADS_A, HEAD_DIM)
            o = attend_a(i, q, k, v) @ w_o_a[i]
            k_rows.append(k)
            v_rows.append(v)
        else:
            j = i - N_A_LAYERS
            if k_s is None:
                ks_, vs_ = jnp.split(x @ w_kv_s, [hkv_b], axis=-1)
                k_s = rope(ks_.reshape(*lead, KV_HEADS_B, HEAD_DIM), pos)
                v_s = vs_.reshape(*lead, KV_HEADS_B, HEAD_DIM)
            q = rope((x @ w_q_b[j]).reshape(*lead, N_HEADS, HEAD_DIM), pos)
            o = attend_b(j, q, k_s, v_s) @ w_o_b[j]
        x = layer_norm(DEEPNORM_ALPHA * x + o, ln_g[i, 0], ln_b[i, 0])
        ffn = grouped_moe(x, w_router, b_router, w_exp_gate[i], w_exp_up[i], w_exp_down[i])
        x = layer_norm(DEEPNORM_ALPHA * x + ffn, ln_g[i, 1], ln_b[i, 1])
        x = x + (p[i] @ w_ple[i]) * jax.nn.sigmoid(x @ w_ple_gate[i])
    return x, jnp.stack(k_rows), jnp.stack(v_rows), k_s, v_s


def setup_inputs(seed: int = 0) -> dict:
    key = jax.random.key(seed)
    ks = jax.random.split(key, 32)
    n_pages = PAST_LEN // PAGE_SIZE
    n_pool = (DEC_BATCH * n_pages * 5) // 4
    wb = min(WINDOW, PAST_LEN)
    hq = N_HEADS * HEAD_DIM
    hkv_a = KV_HEADS_A * HEAD_DIM
    hkv_b = KV_HEADS_B * HEAD_DIM
    sd = D_MODEL ** -0.5

    def nrm(k, shape, scale):
        return jax.random.normal(k, shape, jnp.float32) * scale

    x_prompt = nrm(ks[0], (BATCH, SEQ, D_MODEL), 1.0)
    x_sample = nrm(ks[1], (DEC_BATCH, DEC_SEQ, D_MODEL), 1.0)
    p_prompt = nrm(ks[2], (DEPTH, BATCH, SEQ, PLE_DIM), 1.0)
    p_sample = nrm(ks[3], (DEPTH, DEC_BATCH, DEC_SEQ, PLE_DIM), 1.0)
    cache_k_a = nrm(ks[4], (N_A_LAYERS, n_pool, PAGE_SIZE, KV_HEADS_A, HEAD_DIM), 1.0)
    cache_v_a = nrm(ks[5], (N_A_LAYERS, n_pool, PAGE_SIZE, KV_HEADS_A, HEAD_DIM), 1.0)
    state_swa_k = nrm(ks[6], (DEC_BATCH, wb, KV_HEADS_B, HEAD_DIM), 1.0)
    state_swa_v = nrm(ks[7], (DEC_BATCH, wb, KV_HEADS_B, HEAD_DIM), 1.0)
    page_table = jax.random.permutation(ks[8], n_pool)[: DEC_BATCH * n_pages].reshape(DEC_BATCH, n_pages).astype(jnp.int32)
    w_qkv_a = jnp.concatenate([nrm(ks[9], (N_A_LAYERS, D_MODEL, hq), sd),
                               nrm(ks[10], (N_A_LAYERS, D_MODEL, hkv_a), sd),
                               nrm(ks[11], (N_A_LAYERS, D_MODEL, hkv_a), sd * DEEPNORM_BETA)], axis=-1)
    w_o_a = nrm(ks[12], (N_A_LAYERS, hq, D_MODEL), hq ** -0.5 * DEEPNORM_BETA)
    w_kv_s = jnp.concatenate([nrm(ks[13], (D_MODEL, hkv_b), sd),
                              nrm(ks[14], (D_MODEL, hkv_b), sd * DEEPNORM_BETA)], axis=-1)
    w_q_b = nrm(ks[15], (N_B_LAYERS, D_MODEL, hq), sd)
    w_o_b = nrm(ks[16], (N_B_LAYERS, hq, D_MODEL), hq ** -0.5 * DEEPNORM_BETA)
    sinks_b = nrm(ks[17], (N_B_LAYERS, N_HEADS), 0.5)
    ln_g = 1.0 + nrm(ks[18], (DEPTH, 2, D_MODEL), 0.02)
    ln_b = nrm(ks[19], (DEPTH, 2, D_MODEL), 0.02)
    w_router = nrm(ks[20], (D_MODEL, N_EXPERTS), sd)
    b_router = nrm(ks[21], (N_EXPERTS,), 0.01)
    w_exp_gate = nrm(ks[22], (DEPTH, N_EXPERTS, D_MODEL, D_EXPERT), sd)
    w_exp_up = nrm(ks[23], (DEPTH, N_EXPERTS, D_MODEL, D_EXPERT), sd)
    w_exp_down = nrm(ks[24], (DEPTH, N_EXPERTS, D_EXPERT, D_MODEL), D_EXPERT ** -0.5 * DEEPNORM_BETA)
    w_ple = nrm(ks[25], (DEPTH, PLE_DIM, D_MODEL), PLE_DIM ** -0.5)
    w_ple_gate = nrm(ks[26], (DEPTH, D_MODEL, D_MODEL), sd)
    return {'x_prompt': x_prompt, 'x_sample': x_sample, 'p_prompt': p_prompt, 'p_sample': p_sample,
            'cache_k_a': cache_k_a, 'cache_v_a': cache_v_a, 'state_swa_k': state_swa_k, 'state_swa_v': state_swa_v,
            'page_table': page_table, 'w_qkv_a': w_qkv_a, 'w_o_a': w_o_a, 'w_kv_s': w_kv_s, 'w_q_b': w_q_b,
            'w_o_b': w_o_b, 'sinks_b': sinks_b, 'ln_g': ln_g, 'ln_b': ln_b, 'w_router': w_router,
            'b_router': b_router, 'w_exp_gate': w_exp_gate, 'w_exp_up': w_exp_up, 'w_exp_down': w_exp_down,
            'w_ple': w_ple, 'w_ple_gate': w_ple_gate}


def reference(x_prompt, x_sample, p_prompt, p_sample, cache_k_a, cache_v_a, state_swa_k, state_swa_v, page_table,
              w_qkv_a, w_o_a, w_kv_s, w_q_b, w_o_b, sinks_b, ln_g, ln_b, w_router, b_router,
              w_exp_gate, w_exp_up, w_exp_down, w_ple, w_ple_gate):
    pos_p = jnp.arange(x_prompt.shape[1])
    pos_s = PAST_LEN + jnp.arange(x_sample.shape[1])

    def attend_a_prompt(i, q, k, v):
        return moba_prompt(q, k, v)

    def attend_b_prompt(j, q, k, v):
        return swa_prompt(q, k, v, sinks_b[j])

    def attend_a_sample(i, q, k, v):
        return moba_sample(i, q, k, v, cache_k_a, cache_v_a, page_table)

    def attend_b_sample(j, q, k, v):
        return swa_sample(q, k, v, state_swa_k, state_swa_v, sinks_b[j])

    y_prompt, k_a_prompt, v_a_prompt, k_s_prompt, v_s_prompt = trunk(
        x_prompt, p_prompt, pos_p, attend_a_prompt, attend_b_prompt, w_qkv_a, w_o_a, w_kv_s, w_q_b, w_o_b,
        ln_g, ln_b, w_router, b_router, w_exp_gate, w_exp_up, w_exp_down, w_ple, w_ple_gate)
    y_sample, k_a_sample, v_a_sample, k_s_sample, v_s_sample = trunk(
        x_sample, p_sample, pos_s, attend_a_sample, attend_b_sample, w_qkv_a, w_o_a, w_kv_s, w_q_b, w_o_b,
        ln_g, ln_b, w_router, b_router, w_exp_gate, w_exp_up, w_exp_down, w_ple, w_ple_gate)
    wb_p = min(WINDOW, x_prompt.shape[1])
    swa_k_prompt = k_s_prompt[:, -wb_p:]
    swa_v_prompt = v_s_prompt[:, -wb_p:]
    t = x_sample.shape[1]
    swa_k_sample = jnp.concatenate([state_swa_k, k_s_sample], axis=1)[:, t:]
    swa_v_sample = jnp.concatenate([state_swa_v, v_s_sample], axis=1)[:, t:]
    return (y_prompt, y_sample, k_a_prompt, v_a_prompt, k_a_sample, v_a_sample,
            swa_k_prompt, swa_v_prompt, swa_k_sample, swa_v_sample)
```

```python
import functools

import jax
import jax.numpy as jnp
from jax import lax
from jax.experimental import pallas as pl
from jax.experimental.pallas import tpu as pltpu

D_MODEL = 1024
HEAD_DIM = 64
HALF_DIM = HEAD_DIM // 2
N_HEADS = D_MODEL // HEAD_DIM
KV_HEADS = 4
GROUP = N_HEADS // KV_HEADS
KV_DIM = KV_HEADS * HEAD_DIM
MOBA_BLOCK = 256
BLOCK_SHIFT = MOBA_BLOCK.bit_length() - 1
MOBA_TOPK = 3
WINDOW = 128
PAGE_SIZE = 128
PAST_LEN = 16384
ROPE_THETA = 10000.0
N_EXPERTS = 16
N_GROUPS = 4
EXPERTS_PER_GROUP = N_EXPERTS // N_GROUPS
D_EXPERT = D_MODEL // 2
PLE_DIM = 256
DEPTH = 2
DEEPNORM_ALPHA = (2.0 * DEPTH) ** 0.25
LN_EPS = 1e-5
ATTN_SCALE = HEAD_DIM ** -0.5

LANES = 128
Q_ROWS = GROUP * MOBA_BLOCK
SAMPLE_ROWS = KV_HEADS * GROUP * 4
NEW_PAD = 8
NEW_ROWS = 128
PAGES_PER_ITEM = 32
VMEM_LIMIT = 56 * 1024 * 1024

F32 = jnp.float32
BF16 = jnp.bfloat16
NEG_INF = float("-inf")
NT_DIMS = (((1,), (1,)), ((), ()))


def _nt_dot(a, b):
    return lax.dot_general(a, b, NT_DIMS, preferred_element_type=F32)


def _params(semantics):
    return pltpu.CompilerParams(dimension_semantics=semantics, vmem_limit_bytes=VMEM_LIMIT)


def _layer_norm(z, g, b):
    mu = jnp.mean(z, axis=-1, keepdims=True)
    zc = z - mu
    var = jnp.mean(zc * zc, axis=-1, keepdims=True)
    return zc * lax.rsqrt(var + LN_EPS) * g + b


def _rope_tables(pos):
    inv_freq = ROPE_THETA ** (-jnp.arange(HALF_DIM, dtype=F32) / HALF_DIM)
    ang = pos.astype(F32)[:, None] * inv_freq[None, :]
    cos = jnp.cos(ang)
    sin = jnp.sin(ang)
    cos_t = jnp.tile(cos, (1, 4))
    sin_t = jnp.tile(jnp.concatenate([-sin, sin], axis=1), (1, 2))
    return jnp.concatenate([cos_t, sin_t], axis=1)


def _proj_rope_kernel(x_ref, w_ref, cs_ref, q_ref, k_ref, v_ref, *rest, attn_layout, with_kmean):
    x = x_ref[...].astype(BF16)
    qkv = jnp.dot(x, w_ref[...], preferred_element_type=F32)
    tm = x.shape[0]
    cos = cs_ref[:, :LANES]
    sin = cs_ref[:, LANES:]
    lane = lax.broadcasted_iota(jnp.int32, (tm, LANES), 1)
    first_half = (lane & HALF_DIM) == 0

    def rope(blk):
        partner = jnp.where(first_half, pltpu.roll(blk, LANES - HALF_DIM, 1), pltpu.roll(blk, HALF_DIM, 1))
        return blk * cos + partner * sin

    for j in range(D_MODEL // LANES):
        q_ref[:, j * LANES:(j + 1) * LANES] = (rope(qkv[:, j * LANES:(j + 1) * LANES]) * ATTN_SCALE).astype(BF16)
    k = jnp.concatenate([rope(qkv[:, D_MODEL + j * LANES:D_MODEL + (j + 1) * LANES])
                         for j in range(KV_DIM // LANES)], axis=1)
    v = qkv[:, D_MODEL + KV_DIM:]
    k_ref[...] = k
    v_ref[...] = v
    if attn_layout:
        khm_ref, vt_ref = rest[0], rest[1]
        for h in range(KV_HEADS):
            khm_ref[0, h] = k[:, h * HEAD_DIM:(h + 1) * HEAD_DIM].astype(BF16)
        vt_ref[0] = v.T.astype(BF16)
    if with_kmean:
        km_ref = rest[2]
        nb = tm // MOBA_BLOCK
        km_ref[0] = jnp.sum(k.reshape(nb, MOBA_BLOCK, KV_DIM), axis=1) * (1.0 / MOBA_BLOCK)


def _proj_rope(x2d, w_bf, cs, *, seq, tm, attn_layout, with_kmean):
    t = x2d.shape[0]
    nt = t // tm
    tab_tiles = cs.shape[0] // tm
    tiles_per_seq = seq // tm if attn_layout else 1
    out_shape = [jax.ShapeDtypeStruct((t, D_MODEL), BF16),
                 jax.ShapeDtypeStruct((t, KV_DIM), F32),
                 jax.ShapeDtypeStruct((t, KV_DIM), F32)]
    out_specs = [pl.BlockSpec((tm, D_MODEL), lambda i: (i, 0)),
                 pl.BlockSpec((tm, KV_DIM), lambda i: (i, 0)),
                 pl.BlockSpec((tm, KV_DIM), lambda i: (i, 0))]
    if attn_layout:
        nseq = t // seq
        out_shape += [jax.ShapeDtypeStruct((nseq, KV_HEADS, seq, HEAD_DIM), BF16),
                      jax.ShapeDtypeStruct((nseq, KV_DIM, seq), BF16)]
        out_specs += [pl.BlockSpec((1, KV_HEADS, tm, HEAD_DIM),
                                   lambda i: (i // tiles_per_seq, 0, i % tiles_per_seq, 0)),
                      pl.BlockSpec((1, KV_DIM, tm), lambda i: (i // tiles_per_seq, 0, i % tiles_per_seq))]
    if with_kmean:
        out_shape.append(jax.ShapeDtypeStruct((nt, tm // MOBA_BLOCK, KV_DIM), F32))
        out_specs.append(pl.BlockSpec((1, tm // MOBA_BLOCK, KV_DIM), lambda i: (i, 0, 0)))
    return pl.pallas_call(
        functools.partial(_proj_rope_kernel, attn_layout=attn_layout, with_kmean=with_kmean),
        grid=(nt,),
        in_specs=[pl.BlockSpec((tm, D_MODEL), lambda i: (i, 0)),
                  pl.BlockSpec(w_bf.shape, lambda i: (0, 0)),
                  pl.BlockSpec((tm, 2 * LANES), lambda i: (i % tab_tiles, 0))],
        out_specs=out_specs,
        out_shape=out_shape,
        compiler_params=_params(("parallel",)),
        name="proj_rope",
    )(x2d, w_bf, cs)


def _stack_heads(qb):
    return jnp.concatenate([qb[:, g * HEAD_DIM:(g + 1) * HEAD_DIM] for g in range(GROUP)], axis=0)


def _unstack_out(out_t):
    o2 = jnp.concatenate([out_t[:, g * MOBA_BLOCK:(g + 1) * MOBA_BLOCK] for g in range(GROUP)], axis=0)
    return o2.T.astype(BF16)


def _moba_prompt_kernel(q_ref, k_ref, vt_ref, km_ref, o_ref, sel_ref, m_ref, l_ref, acc_ref):
    n = pl.program_id(2)
    nblk = km_ref.shape[2]
    qrows = _stack_heads(q_ref[0])

    gate = _nt_dot(km_ref[0, 0].astype(BF16), qrows)
    jio = lax.broadcasted_iota(jnp.int32, (nblk, Q_ROWS), 0)
    past = jio < n
    g = jnp.where(past, gate, NEG_INF)
    sel = jnp.zeros((nblk, Q_ROWS), jnp.bool_)
    for _ in range(MOBA_TOPK):
        mx = jnp.max(g, axis=0, keepdims=True)
        idx = jnp.min(jnp.where(g == mx, jio, nblk), axis=0, keepdims=True)
        hit = jio == idx
        sel = sel | hit
        g = jnp.where(hit, NEG_INF, g)
    sel_ref[...] = (sel & past).astype(F32)

    own = pl.multiple_of(n * MOBA_BLOCK, MOBA_BLOCK)
    s = _nt_dot(k_ref[0, 0, pl.ds(own, MOBA_BLOCK), :], qrows)
    kio = lax.broadcasted_iota(jnp.int32, (MOBA_BLOCK, Q_ROWS), 0)
    qio = lax.broadcasted_iota(jnp.int32, (MOBA_BLOCK, Q_ROWS), 1) & (MOBA_BLOCK - 1)
    s = jnp.where(kio <= qio, s, NEG_INF)
    m0 = jnp.max(s, axis=0, keepdims=True)
    p = jnp.exp(s - m0)
    m_ref[...] = m0
    l_ref[...] = jnp.sum(p, axis=0, keepdims=True)
    acc_ref[...] = jnp.dot(vt_ref[0, :, pl.ds(own, MOBA_BLOCK)], p.astype(BF16), preferred_element_type=F32)

    def past_block(j, carry):
        off = pl.multiple_of(j * MOBA_BLOCK, MOBA_BLOCK)
        sj = _nt_dot(k_ref[0, 0, pl.ds(off, MOBA_BLOCK), :], qrows)
        sj = jnp.where(sel_ref[pl.ds(j, 1), :] > 0.0, sj, NEG_INF)
        m_old = m_ref[...]
        m_new = jnp.maximum(m_old, jnp.max(sj, axis=0, keepdims=True))
        a = jnp.exp(m_old - m_new)
        pj = jnp.exp(sj - m_new)
        l_ref[...] = a * l_ref[...] + jnp.sum(pj, axis=0, keepdims=True)
        acc_ref[...] = a * acc_ref[...] + jnp.dot(vt_ref[0, :, pl.ds(off, MOBA_BLOCK)], pj.astype(BF16),
                                                  preferred_element_type=F32)
        m_ref[...] = m_new
        return carry

    lax.fori_loop(0, n, past_block, 0)
    o_ref[0] = _unstack_out(acc_ref[...] / l_ref[...])


def _moba_prompt(q, khm, vt, kmean):
    b, s, _ = q.shape
    nblk = s // MOBA_BLOCK
    return pl.pallas_call(
        _moba_prompt_kernel,
        grid=(b, KV_HEADS, nblk),
        in_specs=[pl.BlockSpec((1, MOBA_BLOCK, KV_DIM), lambda bi, h, n: (bi, n, h)),
                  pl.BlockSpec((1, 1, s, HEAD_DIM), lambda bi, h, n: (bi, h, 0, 0)),
                  pl.BlockSpec((1, HEAD_DIM, s), lambda bi, h, n: (bi, h, 0)),
                  pl.BlockSpec((1, 1, nblk, HEAD_DIM), lambda bi, h, n: (bi, h, 0, 0))],
        out_specs=pl.BlockSpec((1, MOBA_BLOCK, KV_DIM), lambda bi, h, n: (bi, n, h)),
        out_shape=jax.ShapeDtypeStruct((b, s, D_MODEL), BF16),
        scratch_shapes=[pltpu.VMEM((nblk, Q_ROWS), F32),
                        pltpu.VMEM((1, Q_ROWS), F32),
                        pltpu.VMEM((1, Q_ROWS), F32),
                        pltpu.VMEM((HEAD_DIM, Q_ROWS), F32)],
        compiler_params=_params(("parallel", "parallel", "arbitrary")),
        name="moba_prompt",
    )(q, khm, vt, kmean)


SWA_KEYS = MOBA_BLOCK + WINDOW


def _swa_prompt_kernel(sink_ref, q_ref, k_ref, vt_ref, o_ref):
    h = pl.program_id(1)
    n = pl.program_id(2)
    qrows = _stack_heads(q_ref[0])
    start = pl.multiple_of(jnp.maximum(n * MOBA_BLOCK - WINDOW, 0), WINDOW)
    s = _nt_dot(k_ref[0, 0, pl.ds(start, SWA_KEYS), :], qrows)
    kpos = start + lax.broadcasted_iota(jnp.int32, (SWA_KEYS, Q_ROWS), 0)
    lane = lax.broadcasted_iota(jnp.int32, (SWA_KEYS, Q_ROWS), 1)
    diff = n * MOBA_BLOCK + (lane & (MOBA_BLOCK - 1)) - kpos
    valid = (diff >= 0) & (diff < WINDOW)
    grp = lax.broadcasted_iota(jnp.int32, (1, Q_ROWS), 1) >> BLOCK_SHIFT
    sink = jnp.zeros((1, Q_ROWS), F32)
    for g in range(GROUP):
        sink = jnp.where(grp == g, sink_ref[h * GROUP + g], sink)
    m = jnp.maximum(jnp.max(jnp.where(valid, s, NEG_INF), axis=0, keepdims=True), sink)
    e = jnp.where(valid, jnp.exp(s - m), 0.0)
    denom = jnp.sum(e, axis=0, keepdims=True) + jnp.exp(sink - m)
    out_t = jnp.dot(vt_ref[0, :, pl.ds(start, SWA_KEYS)], e.astype(BF16), preferred_element_type=F32)
    o_ref[0] = _unstack_out(out_t / denom)


def _swa_prompt(q, khm, vt, sinks):
    b, s, _ = q.shape
    nblk = s // MOBA_BLOCK
    grid_spec = pltpu.PrefetchScalarGridSpec(
        num_scalar_prefetch=1,
        grid=(b, KV_HEADS, nblk),
        in_specs=[pl.BlockSpec((1, MOBA_BLOCK, KV_DIM), lambda bi, h, n, sk: (bi, n, h)),
                  pl.BlockSpec((1, 1, s, HEAD_DIM), lambda bi, h, n, sk: (bi, h, 0, 0)),
                  pl.BlockSpec((1, HEAD_DIM, s), lambda bi, h, n, sk: (bi, h, 0))],
        out_specs=pl.BlockSpec((1, MOBA_BLOCK, KV_DIM), lambda bi, h, n, sk: (bi, n, h)),
    )
    return pl.pallas_call(
        _swa_prompt_kernel,
        grid_spec=grid_spec,
        out_shape=jax.ShapeDtypeStruct((b, s, D_MODEL), BF16),
        compiler_params=_params(("parallel", "parallel", "arbitrary")),
        name="swa_prompt",
    )(sinks, q, khm, vt)


def _block_diag_q(q):
    db, t, _ = q.shape
    q5 = q.reshape(db, t, KV_HEADS, GROUP, HEAD_DIM).transpose(0, 2, 1, 3, 4)
    q5 = q5.reshape(db, KV_HEADS, t * GROUP, HEAD_DIM)
    eye = jnp.eye(KV_HEADS, dtype=q.dtype)
    return jnp.einsum("ikrd,kc->ikrcd", q5, eye).reshape(db, KV_HEADS * t * GROUP, KV_DIM)


def _block_diag_out(o, t):
    db = o.shape[0]
    o5 = o.reshape(db, KV_HEADS, t * GROUP, KV_HEADS, HEAD_DIM)
    od = jnp.stack([o5[:, h, :, h, :] for h in range(KV_HEADS)], axis=1)
    return od.reshape(db, KV_HEADS, t, GROUP, HEAD_DIM).transpose(0, 2, 1, 3, 4).reshape(db, t, D_MODEL)


def _pad_new(x):
    return jnp.pad(x, ((0, 0), (0, NEW_PAD - x.shape[1]), (0, 0)))


def _row_token(shape):
    return (lax.broadcasted_iota(jnp.int32, shape, 0) >> 2) & 3


def _new_scores(qbd, knew):
    pad = jnp.zeros((NEW_ROWS - NEW_PAD, KV_DIM), BF16)
    s_new = _nt_dot(qbd, jnp.concatenate([knew.astype(BF16), pad], axis=0))
    tcol = lax.broadcasted_iota(jnp.int32, s_new.shape, 1)
    return s_new, tcol <= _row_token(s_new.shape)


def _new_token_part(e_new, v_new):
    out = e_new[:, 0:1] * v_new[0:1, :]
    for t in range(1, 4):
        out = out + e_new[:, t:t + 1] * v_new[t:t + 1, :]
    return out


def _moba_sample_kernel(pt_ref, qbd_ref, knew_ref, vnew_ref, ck_ref, cv_ref, o_ref,
                        buf, sem, s_all, p_all, kmean_s, acc_s):
    i = pl.program_id(0)
    ns = pl.num_programs(0)
    n_pages = pt_ref.shape[1]
    k_items = n_pages // PAGES_PER_ITEM
    n_items = 2 * k_items
    item_rows = PAGES_PER_ITEM * PAGE_SIZE
    blocks_per_item = item_rows // MOBA_BLOCK
    n_blocks = k_items * blocks_per_item
    qbd = qbd_ref[0]

    def page_copy(src, page, slot, pg):
        return pltpu.make_async_copy(src.at[page], buf.at[slot, pl.ds(pg * PAGE_SIZE, PAGE_SIZE), :],
                                     sem.at[slot])

    def start_item(sample, it, slot):
        src = ck_ref if it < k_items else cv_ref
        c = it % k_items
        for pg in range(PAGES_PER_ITEM):
            page_copy(src, pt_ref[sample, c * PAGES_PER_ITEM + pg], slot, pg).start()

    def wait_item(slot):
        for pg in range(PAGES_PER_ITEM):
            page_copy(ck_ref, 0, slot, pg).wait()

    @pl.when(i == 0)
    def _():
        start_item(0, 0, 0)

    if kmean_s.shape[0] > n_blocks:
        kmean_s[n_blocks:, :] = jnp.zeros((kmean_s.shape[0] - n_blocks, KV_DIM), F32)

    own_out = None
    l_tot = None
    for it in range(n_items):
        slot = it % 2
        if it + 1 < n_items:
            start_item(i, it + 1, 1 - slot)
        else:
            @pl.when(i + 1 < ns)
            def _():
                start_item(i + 1, 0, 1 - slot)
        wait_item(slot)

        if it < k_items:
            def k_block(bi, carry, it=it, slot=slot):
                r0 = pl.multiple_of(bi * MOBA_BLOCK, MOBA_BLOCK)
                kb = buf[slot, pl.ds(r0, MOBA_BLOCK), :]
                kmean_s[pl.ds(it * blocks_per_item + bi, 1), :] = (
                    jnp.sum(kb, axis=0, keepdims=True) * (1.0 / MOBA_BLOCK))
                c0 = pl.multiple_of(it * item_rows + bi * MOBA_BLOCK, MOBA_BLOCK)
                s_all[:, pl.ds(c0, MOBA_BLOCK)] = _nt_dot(qbd, kb.astype(BF16))
                return carry
            lax.fori_loop(0, blocks_per_item, k_block, 0)

        if it == k_items - 1:
            nb_pad = kmean_s.shape[0]
            gate = _nt_dot(qbd, kmean_s[...].astype(BF16))
            jio = lax.broadcasted_iota(jnp.int32, (SAMPLE_ROWS, nb_pad), 1)
            g = jnp.where(jio < n_blocks, gate, NEG_INF)
            sel = jnp.zeros((SAMPLE_ROWS, nb_pad), jnp.bool_)
            for _ in range(min(MOBA_TOPK, n_blocks)):
                mx = jnp.max(g, axis=1, keepdims=True)
                idx = jnp.min(jnp.where(g == mx, jio, nb_pad), axis=1, keepdims=True)
                hit = jio == idx
                sel = sel | hit
                g = jnp.where(hit, NEG_INF, g)
            sel_bf = sel.astype(BF16)

            s_new, new_ok = _new_scores(qbd, knew_ref[0])
            m = jnp.max(jnp.where(new_ok, s_new, NEG_INF), axis=1, keepdims=True)

            def expand(c):
                blk = lax.broadcasted_iota(jnp.int32, (nb_pad, item_rows), 0)
                key_blk = lax.broadcasted_iota(jnp.int32, (nb_pad, item_rows), 1) >> BLOCK_SHIFT
                ex = (blk == key_blk + c * blocks_per_item).astype(BF16)
                return jnp.dot(sel_bf, ex, preferred_element_type=F32) > 0.5

            for c in range(k_items):
                sc = jnp.where(expand(c), s_all[:, c * item_rows:(c + 1) * item_rows], NEG_INF)
                m = jnp.maximum(m, jnp.max(sc, axis=1, keepdims=True))
            e_new = jnp.where(new_ok, jnp.exp(s_new - m), 0.0)
            l_tot = jnp.sum(e_new, axis=1, keepdims=True)
            for c in range(k_items):
                sc = jnp.where(expand(c), s_all[:, c * item_rows:(c + 1) * item_rows], NEG_INF)
                pc = jnp.exp(sc - m)
                l_tot = l_tot + jnp.sum(pc, axis=1, keepdims=True)
                p_all[:, c * item_rows:(c + 1) * item_rows] = pc.astype(BF16)
            own_out = _new_token_part(e_new, vnew_ref[0])
            acc_s[...] = jnp.zeros_like(acc_s)

        if it >= k_items:
            def v_block(bi, carry, it=it, slot=slot):
                r0 = pl.multiple_of(bi * MOBA_BLOCK, MOBA_BLOCK)
                vb = buf[slot, pl.ds(r0, MOBA_BLOCK), :].astype(BF16)
                c0 = pl.multiple_of((it - k_items) * item_rows + bi * MOBA_BLOCK, MOBA_BLOCK)
                acc_s[...] += jnp.dot(p_all[:, pl.ds(c0, MOBA_BLOCK)], vb, preferred_element_type=F32)
                return carry
            lax.fori_loop(0, blocks_per_item, v_block, 0)

    o_ref[0] = (acc_s[...] + own_out) / l_tot


def _moba_sample(qbd, k_new, v_new, cache_k, cache_v, page_table):
    db = qbd.shape[0]
    n_pages = page_table.shape[1]
    past = n_pages * PAGE_SIZE
    item_rows = PAGES_PER_ITEM * PAGE_SIZE
    grid_spec = pltpu.PrefetchScalarGridSpec(
        num_scalar_prefetch=1,
        grid=(db,),
        in_specs=[pl.BlockSpec((1, SAMPLE_ROWS, KV_DIM), lambda i, pt: (i, 0, 0)),
                  pl.BlockSpec((1, NEW_PAD, KV_DIM), lambda i, pt: (i, 0, 0)),
                  pl.BlockSpec((1, NEW_PAD, KV_DIM), lambda i, pt: (i, 0, 0)),
                  pl.BlockSpec(memory_space=pl.ANY),
                  pl.BlockSpec(memory_space=pl.ANY)],
        out_specs=pl.BlockSpec((1, SAMPLE_ROWS, KV_DIM), lambda i, pt: (i, 0, 0)),
        scratch_shapes=[pltpu.VMEM((2, item_rows, KV_DIM), F32),
                        pltpu.SemaphoreType.DMA((2,)),
                        pltpu.VMEM((SAMPLE_ROWS, past), F32),
                        pltpu.VMEM((SAMPLE_ROWS, past), BF16),
                        pltpu.VMEM((-(-(past // MOBA_BLOCK) // LANES) * LANES, KV_DIM), F32),
                        pltpu.VMEM((SAMPLE_ROWS, KV_DIM), F32)],
    )
    return pl.pallas_call(
        _moba_sample_kernel,
        grid_spec=grid_spec,
        out_shape=jax.ShapeDtypeStruct((db, SAMPLE_ROWS, KV_DIM), F32),
        compiler_params=_params(("arbitrary",)),
        name="moba_sample",
    )(page_table, qbd, k_new, v_new, cache_k, cache_v)


def _swa_sample_kernel(qbd_ref, kbuf_ref, vbuf_ref, knew_ref, vnew_ref, sink_ref, o_ref):
    qbd = qbd_ref[0]
    wb = kbuf_ref.shape[1]
    s_buf = _nt_dot(qbd, kbuf_ref[0].astype(BF16))
    s_new, new_ok = _new_scores(qbd, knew_ref[0])
    bcol = lax.broadcasted_iota(jnp.int32, (SAMPLE_ROWS, wb), 1)
    bdiff = wb + _row_token((SAMPLE_ROWS, wb)) - bcol
    buf_ok = (bdiff >= 0) & (bdiff < WINDOW)
    sink = sink_ref[:, 0:1]
    m = jnp.maximum(jnp.max(jnp.where(buf_ok, s_buf, NEG_INF), axis=1, keepdims=True),
                    jnp.max(jnp.where(new_ok, s_new, NEG_INF), axis=1, keepdims=True))
    m = jnp.maximum(m, sink)
    e_buf = jnp.where(buf_ok, jnp.exp(s_buf - m), 0.0)
    e_new = jnp.where(new_ok, jnp.exp(s_new - m), 0.0)
    denom = jnp.sum(e_buf, axis=1, keepdims=True) + jnp.sum(e_new, axis=1, keepdims=True) + jnp.exp(sink - m)
    out = jnp.dot(e_buf.astype(BF16), vbuf_ref[0].astype(BF16), preferred_element_type=F32)
    o_ref[0] = (out + _new_token_part(e_new, vnew_ref[0])) / denom


def _swa_sample(qbd, k_buf, v_buf, k_new, v_new, sink_rows):
    db, wb, _ = k_buf.shape
    return pl.pallas_call(
        _swa_sample_kernel,
        grid=(db,),
        in_specs=[pl.BlockSpec((1, SAMPLE_ROWS, KV_DIM), lambda i: (i, 0, 0)),
                  pl.BlockSpec((1, wb, KV_DIM), lambda i: (i, 0, 0)),
                  pl.BlockSpec((1, wb, KV_DIM), lambda i: (i, 0, 0)),
                  pl.BlockSpec((1, NEW_PAD, KV_DIM), lambda i: (i, 0, 0)),
                  pl.BlockSpec((1, NEW_PAD, KV_DIM), lambda i: (i, 0, 0)),
                  pl.BlockSpec((SAMPLE_ROWS, LANES), lambda i: (0, 0))],
        out_specs=pl.BlockSpec((1, SAMPLE_ROWS, KV_DIM), lambda i: (i, 0, 0)),
        out_shape=jax.ShapeDtypeStruct((db, SAMPLE_ROWS, KV_DIM), F32),
        compiler_params=_params(("parallel",)),
        name="swa_sample",
    )(qbd, k_buf, v_buf, k_new, v_new, sink_rows)


def _route(aff, bias):
    shape = aff.shape
    lane = lax.broadcasted_iota(jnp.int32, shape, 1)
    biased = jnp.where(lane < N_EXPERTS, aff + bias, NEG_INF)
    best = bi1 = bi2 = None
    for grp in range(N_GROUPS):
        in_grp = (lane >= grp * EXPERTS_PER_GROUP) & (lane < (grp + 1) * EXPERTS_PER_GROUP)
        v = jnp.where(in_grp, biased, NEG_INF)
        m1 = jnp.max(v, axis=1, keepdims=True)
        i1 = jnp.min(jnp.where(v == m1, lane, LANES), axis=1, keepdims=True)
        v2 = jnp.where(lane == i1, NEG_INF, v)
        m2 = jnp.max(v2, axis=1, keepdims=True)
        i2 = jnp.min(jnp.where(v2 == m2, lane, LANES), axis=1, keepdims=True)
        gs = m1 + m2
        if grp == 0:
            best, bi1, bi2 = gs, i1, i2
        else:
            better = gs > best
            best = jnp.where(better, gs, best)
            bi1 = jnp.where(better, i1, bi1)
            bi2 = jnp.where(better, i2, bi2)
    sel1 = lane == bi1
    sel2 = lane == bi2
    a1 = jnp.sum(jnp.where(sel1, aff, 0.0), axis=1, keepdims=True)
    a2 = jnp.sum(jnp.where(sel2, aff, 0.0), axis=1, keepdims=True)
    den = a1 + a2
    return jnp.where(sel1, a1 / den, 0.0) + jnp.where(sel2, a2 / den, 0.0)


def _post_attn_kernel(a_ref, x_ref, wo_ref, g_ref, b_ref, wr_ref, br_ref, x1_ref, gates_ref):
    o = jnp.dot(a_ref[...], wo_ref[...], preferred_element_type=F32)
    x1 = _layer_norm(DEEPNORM_ALPHA * x_ref[...] + o, g_ref[...], b_ref[...])
    x1_ref[...] = x1
    logits = jnp.dot(x1, wr_ref[...], preferred_element_type=F32, precision=lax.Precision.HIGHEST)
    gates_ref[...] = _route(jax.nn.sigmoid(logits), br_ref[...])


def _post_attn(attn, x2d, wo_bf, ln_g, ln_b, wr_pad, br_pad, *, tm):
    t = x2d.shape[0]
    return pl.pallas_call(
        _post_attn_kernel,
        grid=(t // tm,),
        in_specs=[pl.BlockSpec((tm, D_MODEL), lambda i: (i, 0)),
                  pl.BlockSpec((tm, D_MODEL), lambda i: (i, 0)),
                  pl.BlockSpec((D_MODEL, D_MODEL), lambda i: (0, 0)),
                  pl.BlockSpec((1, D_MODEL), lambda i: (0, 0)),
                  pl.BlockSpec((1, D_MODEL), lambda i: (0, 0)),
                  pl.BlockSpec((D_MODEL, LANES), lambda i: (0, 0)),
                  pl.BlockSpec((1, LANES), lambda i: (0, 0))],
        out_specs=[pl.BlockSpec((tm, D_MODEL), lambda i: (i, 0)),
                   pl.BlockSpec((tm, LANES), lambda i: (i, 0))],
        out_shape=[jax.ShapeDtypeStruct((t, D_MODEL), F32),
                   jax.ShapeDtypeStruct((t, LANES), F32)],
        compiler_params=_params(("parallel",)),
        name="post_attn",
    )(attn, x2d, wo_bf, ln_g, ln_b, wr_pad, br_pad)


def _moe_kernel(x1_ref, gates_ref, wg_ref, wu_ref, wd_ref, g_ref, b_ref, p_ref, wple_ref, wpg_ref,
                out_ref, xb_ref, acc_ref):
    e = pl.program_id(1)

    @pl.when(e == 0)
    def _():
        xb_ref[...] = x1_ref[...].astype(BF16)
        acc_ref[...] = jnp.zeros_like(acc_ref)

    xb = xb_ref[...]
    hg = jnp.dot(xb, wg_ref[0], preferred_element_type=F32)
    hu = jnp.dot(xb, wu_ref[0], preferred_element_type=F32)
    h = hg * jax.nn.sigmoid(hg) * hu
    y = jnp.dot(h.astype(BF16), wd_ref[0], preferred_element_type=F32)
    gates = gates_ref[...]
    lane = lax.broadcasted_iota(jnp.int32, gates.shape, 1)
    gate_e = jnp.sum(jnp.where(lane == e, gates, 0.0), axis=1, keepdims=True)
    acc_ref[...] += gate_e * y

    @pl.when(e == pl.num_programs(1) - 1)
    def _():
        x2 = _layer_norm(DEEPNORM_ALPHA * x1_ref[...] + acc_ref[...], g_ref[...], b_ref[...])
        ple = jnp.dot(p_ref[...].astype(BF16), wple_ref[...], preferred_element_type=F32)
        gt = jax.nn.sigmoid(jnp.dot(x2.astype(BF16), wpg_ref[...], preferred_element_type=F32))
        out_ref[...] = x2 + ple * gt


def _moe(x1, gates, wg_bf, wu_bf, wd_bf, ln_g, ln_b, p2d, wple_bf, wpg_bf, *, tm):
    t = x1.shape[0]
    return pl.pallas_call(
        _moe_kernel,
        grid=(t // tm, N_EXPERTS),
        in_specs=[pl.BlockSpec((tm, D_MODEL), lambda i, e: (i, 0)),
                  pl.BlockSpec((tm, LANES), lambda i, e: (i, 0)),
                  pl.BlockSpec((1, D_MODEL, D_EXPERT), lambda i, e: (e, 0, 0)),
                  pl.BlockSpec((1, D_MODEL, D_EXPERT), lambda i, e: (e, 0, 0)),
                  pl.BlockSpec((1, D_EXPERT, D_MODEL), lambda i, e: (e, 0, 0)),
                  pl.BlockSpec((1, D_MODEL), lambda i, e: (0, 0)),
                  pl.BlockSpec((1, D_MODEL), lambda i, e: (0, 0)),
                  pl.BlockSpec((tm, PLE_DIM), lambda i, e: (i, 0)),
                  pl.BlockSpec((PLE_DIM, D_MODEL), lambda i, e: (0, 0)),
                  pl.BlockSpec((D_MODEL, D_MODEL), lambda i, e: (0, 0))],
        out_specs=pl.BlockSpec((tm, D_MODEL), lambda i, e: (i, 0)),
        out_shape=jax.ShapeDtypeStruct((t, D_MODEL), F32),
        scratch_shapes=[pltpu.VMEM((tm, D_MODEL), BF16),
                        pltpu.VMEM((tm, D_MODEL), F32)],
        compiler_params=_params(("parallel", "arbitrary")),
        name="moe",
    )(x1, gates, wg_bf, wu_bf, wd_bf, ln_g, ln_b, p2d, wple_bf, wpg_bf)


def _token_tile(t):
    return 512 if t % 512 == 0 else t


def _ffn_block(attn2d, x2d, p2d, layer, wts, *, tm, tm_moe):
    x1, gates = _post_attn(attn2d, x2d, wts["w_o"][layer], wts["ln_g"][layer, 0:1], wts["ln_b"][layer, 0:1],
                           wts["w_router"], wts["b_router"], tm=tm)
    return _moe(x1, gates, wts["w_gate"][layer], wts["w_up"][layer], wts["w_down"][layer],
                wts["ln_g"][layer, 1:2], wts["ln_b"][layer, 1:2], p2d, wts["w_ple"][layer],
                wts["w_ple_gate"][layer], tm=tm_moe)


def _prompt_trunk(x, p, wts):
    b, s, _ = x.shape
    t = b * s
    tm = _token_tile(s)
    tm_moe = 1024 if t % 1024 == 0 else tm
    cs = _rope_tables(jnp.arange(s))
    x2d = x.reshape(t, D_MODEL)

    q, k0, v0, khm, vt, km = _proj_rope(x2d, wts["w_qkv"][0], cs, seq=s, tm=tm, attn_layout=True, with_kmean=True)
    kmean = km.reshape(b, s // MOBA_BLOCK, KV_HEADS, HEAD_DIM).transpose(0, 2, 1, 3)
    attn = _moba_prompt(q.reshape(b, s, D_MODEL), khm, vt, kmean)
    x2d = _ffn_block(attn.reshape(t, D_MODEL), x2d, p[0].reshape(t, PLE_DIM), 0, wts, tm=tm, tm_moe=tm_moe)

    q, k1, v1, khm, vt = _proj_rope(x2d, wts["w_qkv"][1], cs, seq=s, tm=tm, attn_layout=True, with_kmean=False)
    attn = _swa_prompt(q.reshape(b, s, D_MODEL), khm, vt, wts["sinks"])
    x2d = _ffn_block(attn.reshape(t, D_MODEL), x2d, p[1].reshape(t, PLE_DIM), 1, wts, tm=tm, tm_moe=tm_moe)
    return (x2d.reshape(b, s, D_MODEL), k0.reshape(b, s, KV_HEADS, HEAD_DIM), v0.reshape(b, s, KV_HEADS, HEAD_DIM),
            k1.reshape(b, s, KV_HEADS, HEAD_DIM), v1.reshape(b, s, KV_HEADS, HEAD_DIM))


def _sample_trunk(x, p, cache_k, cache_v, state_k, state_v, page_table, wts):
    db, tn, _ = x.shape
    t = db * tn
    tm = _token_tile(t)
    pos = PAST_LEN + jnp.arange(tn)
    cs = jnp.tile(_rope_tables(pos), (db, 1))
    x2d = x.reshape(t, D_MODEL)
    pool = cache_k.shape[1]

    q, k0, v0 = _proj_rope(x2d, wts["w_qkv"][0], cs, seq=tn, tm=tm, attn_layout=False, with_kmean=False)
    o = _moba_sample(_block_diag_q(q.reshape(db, tn, D_MODEL)),
                     _pad_new(k0.reshape(db, tn, KV_DIM)), _pad_new(v0.reshape(db, tn, KV_DIM)),
                     cache_k.reshape(pool, PAGE_SIZE, KV_DIM), cache_v.reshape(pool, PAGE_SIZE, KV_DIM),
                     page_table)
    attn = _block_diag_out(o, tn).astype(BF16)
    x2d = _ffn_block(attn.reshape(t, D_MODEL), x2d, p[0].reshape(t, PLE_DIM), 0, wts, tm=tm, tm_moe=tm)

    q, k1, v1 = _proj_rope(x2d, wts["w_qkv"][1], cs, seq=tn, tm=tm, attn_layout=False, with_kmean=False)
    wb = state_k.shape[1]
    sink_rows = jnp.broadcast_to(
        jnp.broadcast_to(wts["sinks"].reshape(KV_HEADS, 1, GROUP), (KV_HEADS, tn, GROUP)).reshape(SAMPLE_ROWS, 1),
        (SAMPLE_ROWS, LANES))
    o = _swa_sample(_block_diag_q(q.reshape(db, tn, D_MODEL)),
                    state_k.reshape(db, wb, KV_DIM), state_v.reshape(db, wb, KV_DIM),
                    _pad_new(k1.reshape(db, tn, KV_DIM)), _pad_new(v1.reshape(db, tn, KV_DIM)), sink_rows)
    attn = _block_diag_out(o, tn).astype(BF16)
    x2d = _ffn_block(attn.reshape(t, D_MODEL), x2d, p[1].reshape(t, PLE_DIM), 1, wts, tm=tm, tm_moe=tm)
    return (x2d.reshape(db, tn, D_MODEL), k0.reshape(db, tn, KV_HEADS, HEAD_DIM), v0.reshape(db, tn, KV_HEADS, HEAD_DIM),
            k1.reshape(db, tn, KV_HEADS, HEAD_DIM), v1.reshape(db, tn, KV_HEADS, HEAD_DIM))


def kernel(x_prompt, x_sample, p_prompt, p_sample, cache_k_a, cache_v_a, state_swa_k, state_swa_v, page_table,
           w_qkv_a, w_o_a, w_kv_s, w_q_b, w_o_b, sinks_b, ln_g, ln_b, w_router, b_router,
           w_exp_gate, w_exp_up, w_exp_down, w_ple, w_ple_gate):
    wts = {
        "w_qkv": jnp.stack([w_qkv_a[0], jnp.concatenate([w_q_b[0], w_kv_s], axis=1)]).astype(BF16),
        "w_o": jnp.stack([w_o_a[0], w_o_b[0]]).astype(BF16),
        "sinks": sinks_b[0].astype(F32),
        "ln_g": ln_g, "ln_b": ln_b,
        "w_router": jnp.pad(w_router.astype(F32), ((0, 0), (0, LANES - N_EXPERTS))),
        "b_router": jnp.pad(b_router.astype(F32), (0, LANES - N_EXPERTS)).reshape(1, LANES),
        "w_gate": w_exp_gate.astype(BF16), "w_up": w_exp_up.astype(BF16), "w_down": w_exp_down.astype(BF16),
        "w_ple": w_ple.astype(BF16), "w_ple_gate": w_ple_gate.astype(BF16),
    }
    y_p, ka_p, va_p, ks_p, vs_p = _prompt_trunk(x_prompt, p_prompt, wts)
    y_s, ka_s, va_s, ks_s, vs_s = _sample_trunk(x_sample, p_sample, cache_k_a, cache_v_a,
                                                state_swa_k, state_swa_v, page_table, wts)
    wb_p = min(WINDOW, x_prompt.shape[1])
    tn = x_sample.shape[1]
    return (y_p, y_s, ka_p[None], va_p[None], ka_s[None], va_s[None],
            ks_p[:, -wb_p:], vs_p[:, -wb_p:],
            jnp.concatenate([state_swa_k, ks_s], axis=1)[:, tn:],
            jnp.concatenate([state_swa_v, vs_s], axis=1)[:, tn:])
```

```python
import functools

import jax
import jax.numpy as jnp
from jax import lax
from jax.experimental import pallas as pl
from jax.experimental.pallas import tpu as pltpu

D_MODEL = 1024
HEAD_DIM = 64
HALF_DIM = HEAD_DIM // 2
N_HEADS = D_MODEL // HEAD_DIM
KV_HEADS = 4
GROUP = N_HEADS // KV_HEADS
KV_DIM = KV_HEADS * HEAD_DIM
MOBA_BLOCK = 256
BLOCK_SHIFT = MOBA_BLOCK.bit_length() - 1
MOBA_TOPK = 3
WINDOW = 128
PAGE_SIZE = 128
PAST_LEN = 16384
ROPE_THETA = 10000.0
N_EXPERTS = 16
N_GROUPS = 4
EXPERTS_PER_GROUP = N_EXPERTS // N_GROUPS
D_EXPERT = D_MODEL // 2
PLE_DIM = 256
DEPTH = 2
DEEPNORM_ALPHA = (2.0 * DEPTH) ** 0.25
LN_EPS = 1e-5
ATTN_SCALE = HEAD_DIM ** -0.5
LOG2E = 1.4426950408889634
Q_SCALE = ATTN_SCALE * LOG2E

LANES = 128
Q_ROWS = GROUP * MOBA_BLOCK
SAMPLE_ROWS = KV_HEADS * GROUP * 4
NEW_PAD = 8
NEW_ROWS = 128
PAGES_PER_ITEM = 32
VMEM_LIMIT = 56 * 1024 * 1024

F32 = jnp.float32
BF16 = jnp.bfloat16
NEG_INF = float("-inf")
NT_DIMS = (((1,), (1,)), ((), ()))


def _nt_dot(a, b):
    return lax.dot_general(a, b, NT_DIMS, preferred_element_type=F32)


def _params(semantics):
    return pltpu.CompilerParams(dimension_semantics=semantics, vmem_limit_bytes=VMEM_LIMIT)


def _layer_norm(z, g, b):
    mu = jnp.mean(z, axis=-1, keepdims=True)
    zc = z - mu
    var = jnp.mean(zc * zc, axis=-1, keepdims=True)
    return zc * lax.rsqrt(var + LN_EPS) * g + b


def _rope_tables(pos):
    inv_freq = ROPE_THETA ** (-jnp.arange(HALF_DIM, dtype=F32) / HALF_DIM)
    ang = pos.astype(F32)[:, None] * inv_freq[None, :]
    cos = jnp.cos(ang)
    sin = jnp.sin(ang)
    cos_t = jnp.tile(cos, (1, 4))
    sin_t = jnp.tile(jnp.concatenate([-sin, sin], axis=1), (1, 2))
    return jnp.concatenate([cos_t, sin_t], axis=1)


def _proj_rope_kernel(x_ref, w_ref, cs_ref, q_ref, k_ref, v_ref, *rest, attn_layout, with_kmean):
    x = x_ref[...].astype(BF16)
    qkv = jnp.dot(x, w_ref[...], preferred_element_type=F32)
    tm = x.shape[0]
    cos = cs_ref[:, :LANES]
    sin = cs_ref[:, LANES:]
    lane = lax.broadcasted_iota(jnp.int32, (tm, LANES), 1)
    first_half = (lane & HALF_DIM) == 0

    def rope(blk):
        partner = jnp.where(first_half, pltpu.roll(blk, LANES - HALF_DIM, 1), pltpu.roll(blk, HALF_DIM, 1))
        return blk * cos + partner * sin

    for j in range(D_MODEL // LANES):
        q_ref[:, j * LANES:(j + 1) * LANES] = (rope(qkv[:, j * LANES:(j + 1) * LANES]) * Q_SCALE).astype(BF16)
    k = jnp.concatenate([rope(qkv[:, D_MODEL + j * LANES:D_MODEL + (j + 1) * LANES])
                         for j in range(KV_DIM // LANES)], axis=1)
    v = qkv[:, D_MODEL + KV_DIM:]
    k_ref[...] = k
    v_ref[...] = v
    if attn_layout:
        khm_ref, vt_ref = rest[0], rest[1]
        for h in range(KV_HEADS):
            khm_ref[0, h] = k[:, h * HEAD_DIM:(h + 1) * HEAD_DIM].astype(BF16)
        vt_ref[0] = v.T.astype(BF16)
    if with_kmean:
        km_ref = rest[2]
        nb = tm // MOBA_BLOCK
        km_ref[0] = jnp.sum(k.reshape(nb, MOBA_BLOCK, KV_DIM), axis=1) * (1.0 / MOBA_BLOCK)


def _proj_rope(x2d, w_bf, cs, *, seq, tm, attn_layout, with_kmean):
    t = x2d.shape[0]
    nt = t // tm
    tab_tiles = cs.shape[0] // tm
    tiles_per_seq = seq // tm if attn_layout else 1
    out_shape = [jax.ShapeDtypeStruct((t, D_MODEL), BF16),
                 jax.ShapeDtypeStruct((t, KV_DIM), F32),
                 jax.ShapeDtypeStruct((t, KV_DIM), F32)]
    out_specs = [pl.BlockSpec((tm, D_MODEL), lambda i: (i, 0)),
                 pl.BlockSpec((tm, KV_DIM), lambda i: (i, 0)),
                 pl.BlockSpec((tm, KV_DIM), lambda i: (i, 0))]
    if attn_layout:
        nseq = t // seq
        out_shape += [jax.ShapeDtypeStruct((nseq, KV_HEADS, seq, HEAD_DIM), BF16),
                      jax.ShapeDtypeStruct((nseq, KV_DIM, seq), BF16)]
        out_specs += [pl.BlockSpec((1, KV_HEADS, tm, HEAD_DIM),
                                   lambda i: (i // tiles_per_seq, 0, i % tiles_per_seq, 0)),
                      pl.BlockSpec((1, KV_DIM, tm), lambda i: (i // tiles_per_seq, 0, i % tiles_per_seq))]
    if with_kmean:
        out_shape.append(jax.ShapeDtypeStruct((nt, tm // MOBA_BLOCK, KV_DIM), F32))
        out_specs.append(pl.BlockSpec((1, tm // MOBA_BLOCK, KV_DIM), lambda i: (i, 0, 0)))
    return pl.pallas_call(
        functools.partial(_proj_rope_kernel, attn_layout=attn_layout, with_kmean=with_kmean),
        grid=(nt,),
        in_specs=[pl.BlockSpec((tm, D_MODEL), lambda i: (i, 0)),
                  pl.BlockSpec(w_bf.shape, lambda i: (0, 0)),
                  pl.BlockSpec((tm, 2 * LANES), lambda i: (i % tab_tiles, 0))],
        out_specs=out_specs,
        out_shape=out_shape,
        compiler_params=_params(("parallel",)),
        name="proj_rope",
    )(x2d, w_bf, cs)


def _stack_heads(qb):
    return jnp.concatenate([qb[:, g * HEAD_DIM:(g + 1) * HEAD_DIM] for g in range(GROUP)], axis=0)


def _unstack_out(out_t):
    o2 = jnp.concatenate([out_t[:, g * MOBA_BLOCK:(g + 1) * MOBA_BLOCK] for g in range(GROUP)], axis=0)
    return o2.T.astype(BF16)


def _lane_chunks():
    return [slice(c * LANES, (c + 1) * LANES) for c in range(Q_ROWS // LANES)]


def _moba_prompt_kernel(q_ref, k_ref, vt_ref, km_ref, o_ref, sel_ref, m_ref, l_ref, a_ref, p_ref, acc_ref):
    n = pl.program_id(2)
    nblk = km_ref.shape[2]
    qrows = _stack_heads(q_ref[0])

    gate = _nt_dot(km_ref[0, 0].astype(BF16), qrows)
    jio = lax.broadcasted_iota(jnp.int32, (nblk, Q_ROWS), 0)
    past = jio < n
    g = jnp.where(past, gate, NEG_INF)
    sel = jnp.zeros((nblk, Q_ROWS), jnp.bool_)
    for _ in range(MOBA_TOPK):
        mx = jnp.max(g, axis=0, keepdims=True)
        idx = jnp.min(jnp.where(g == mx, jio, nblk), axis=0, keepdims=True)
        hit = jio == idx
        sel = sel | hit
        g = jnp.where(hit, NEG_INF, g)
    sel_ref[...] = (sel & past).astype(F32)

    own = pl.multiple_of(n * MOBA_BLOCK, MOBA_BLOCK)
    k_own = k_ref[0, 0, pl.ds(own, MOBA_BLOCK), :]
    kio = lax.broadcasted_iota(jnp.int32, (MOBA_BLOCK, LANES), 0)
    lio = lax.broadcasted_iota(jnp.int32, (MOBA_BLOCK, LANES), 1)
    for ci, cs in enumerate(_lane_chunks()):
        s = _nt_dot(k_own, qrows[cs, :])
        qio = (lio + ci * LANES) & (MOBA_BLOCK - 1)
        s = jnp.where(kio <= qio, s, NEG_INF)
        m0 = jnp.max(s, axis=0, keepdims=True)
        p = jnp.exp2(s - m0)
        m_ref[:, cs] = m0
        l_ref[:, cs] = jnp.sum(p, axis=0, keepdims=True)
        p_ref[:, cs] = p.astype(BF16)
    acc_ref[...] = jnp.zeros_like(acc_ref)

    def pv(blk):
        off = pl.multiple_of(blk * MOBA_BLOCK, MOBA_BLOCK)
        return jnp.dot(vt_ref[0, :, pl.ds(off, MOBA_BLOCK)], p_ref[...], preferred_element_type=F32)

    def past_block(j):
        pv_prev = pv(jnp.where(j == 0, n, j - 1))
        off = pl.multiple_of(j * MOBA_BLOCK, MOBA_BLOCK)
        k_j = k_ref[0, 0, pl.ds(off, MOBA_BLOCK), :]
        sel_j = sel_ref[pl.ds(j, 1), :]
        for cs in _lane_chunks():
            sj = _nt_dot(k_j, qrows[cs, :])
            picked = sel_j[:, cs] > 0.0
            m_old = m_ref[:, cs]
            m_new = jnp.where(picked, jnp.maximum(m_old, jnp.max(sj, axis=0, keepdims=True)), m_old)
            a = jnp.exp2(m_old - m_new)
            pj = jnp.exp2(sj - jnp.where(picked, m_new, jnp.inf))
            l_ref[:, cs] = a * l_ref[:, cs] + jnp.sum(pj, axis=0, keepdims=True)
            m_ref[:, cs] = m_new
            a_ref[:, cs] = a
            p_ref[:, cs] = pj.astype(BF16)
        acc_ref[...] = a_ref[...] * (acc_ref[...] + pv_prev)

    odd = n & 1

    @pl.when(odd == 1)
    def _():
        past_block(0)

    def block_pair(i, carry):
        past_block(2 * i + odd)
        past_block(2 * i + odd + 1)
        return carry

    lax.fori_loop(0, n >> 1, block_pair, 0)
    last = jnp.where(n == 0, 0, n - 1)
    o_ref[0] = _unstack_out((acc_ref[...] + pv(last)) / l_ref[...])


def _moba_prompt(q, khm, vt, kmean):
    b, s, _ = q.shape
    nblk = s // MOBA_BLOCK
    return pl.pallas_call(
        _moba_prompt_kernel,
        grid=(b, KV_HEADS, nblk),
        in_specs=[pl.BlockSpec((1, MOBA_BLOCK, KV_DIM), lambda bi, h, n: (bi, n, h)),
                  pl.BlockSpec((1, 1, s, HEAD_DIM), lambda bi, h, n: (bi, h, 0, 0)),
                  pl.BlockSpec((1, HEAD_DIM, s), lambda bi, h, n: (bi, h, 0)),
                  pl.BlockSpec((1, 1, nblk, HEAD_DIM), lambda bi, h, n: (bi, h, 0, 0))],
        out_specs=pl.BlockSpec((1, MOBA_BLOCK, KV_DIM), lambda bi, h, n: (bi, n, h)),
        out_shape=jax.ShapeDtypeStruct((b, s, D_MODEL), BF16),
        scratch_shapes=[pltpu.VMEM((nblk, Q_ROWS), F32),
                        pltpu.VMEM((1, Q_ROWS), F32),
                        pltpu.VMEM((1, Q_ROWS), F32),
                        pltpu.VMEM((1, Q_ROWS), F32),
                        pltpu.VMEM((MOBA_BLOCK, Q_ROWS), BF16),
                        pltpu.VMEM((HEAD_DIM, Q_ROWS), F32)],
        compiler_params=_params(("parallel", "parallel", "arbitrary")),
        name="moba_prompt",
    )(q, khm, vt, kmean)


SWA_KEYS = MOBA_BLOCK + WINDOW


def _swa_prompt_kernel(sink_ref, q_ref, k_ref, vt_ref, o_ref, d_ref, p_ref, bias_ref):
    h = pl.program_id(1)
    n = pl.program_id(2)
    qrows = _stack_heads(q_ref[0])
    start = pl.multiple_of(jnp.maximum(n * MOBA_BLOCK - WINDOW, 0), WINDOW)
    k_win = k_ref[0, 0, pl.ds(start, SWA_KEYS), :]

    @pl.when(n <= 1)
    def _():
        kpos = start + lax.broadcasted_iota(jnp.int32, (SWA_KEYS, LANES), 0)
        lio = lax.broadcasted_iota(jnp.int32, (SWA_KEYS, LANES), 1)
        for ci, cs in enumerate(_lane_chunks()):
            diff = n * MOBA_BLOCK + ((lio + ci * LANES) & (MOBA_BLOCK - 1)) - kpos
            bias_ref[:, cs] = jnp.where((diff >= 0) & (diff < WINDOW), 0.0, NEG_INF)

    for ci, cs in enumerate(_lane_chunks()):
        s = _nt_dot(k_win, qrows[cs, :]) + bias_ref[:, cs]
        sink = sink_ref[h * GROUP + (ci * LANES) // MOBA_BLOCK] * LOG2E
        m = jnp.maximum(jnp.max(s, axis=0, keepdims=True), sink)
        e = jnp.exp2(s - m)
        d_ref[:, cs] = jnp.sum(e, axis=0, keepdims=True) + jnp.exp2(sink - m)
        p_ref[:, cs] = e.astype(BF16)
    out_t = jnp.dot(vt_ref[0, :, pl.ds(start, SWA_KEYS)], p_ref[...], preferred_element_type=F32)
    o_ref[0] = _unstack_out(out_t / d_ref[...])


def _swa_prompt(q, khm, vt, sinks):
    b, s, _ = q.shape
    nblk = s // MOBA_BLOCK
    grid_spec = pltpu.PrefetchScalarGridSpec(
        num_scalar_prefetch=1,
        grid=(b, KV_HEADS, nblk),
        in_specs=[pl.BlockSpec((1, MOBA_BLOCK, KV_DIM), lambda bi, h, n, sk: (bi, n, h)),
                  pl.BlockSpec((1, 1, s, HEAD_DIM), lambda bi, h, n, sk: (bi, h, 0, 0)),
                  pl.BlockSpec((1, HEAD_DIM, s), lambda bi, h, n, sk: (bi, h, 0))],
        out_specs=pl.BlockSpec((1, MOBA_BLOCK, KV_DIM), lambda bi, h, n, sk: (bi, n, h)),
        scratch_shapes=[pltpu.VMEM((1, Q_ROWS), F32),
                        pltpu.VMEM((SWA_KEYS, Q_ROWS), BF16),
                        pltpu.VMEM((SWA_KEYS, Q_ROWS), F32)],
    )
    return pl.pallas_call(
        _swa_prompt_kernel,
        grid_spec=grid_spec,
        out_shape=jax.ShapeDtypeStruct((b, s, D_MODEL), BF16),
        compiler_params=_params(("parallel", "parallel", "arbitrary")),
        name="swa_prompt",
    )(sinks, q, khm, vt)


def _block_diag_q(q):
    db, t, _ = q.shape
    q5 = q.reshape(db, t, KV_HEADS, GROUP, HEAD_DIM).transpose(0, 2, 1, 3, 4)
    q5 = q5.reshape(db, KV_HEADS, t * GROUP, HEAD_DIM)
    eye = jnp.eye(KV_HEADS, dtype=q.dtype)
    return jnp.einsum("ikrd,kc->ikrcd", q5, eye).reshape(db, KV_HEADS * t * GROUP, KV_DIM)


def _block_diag_out(o, t):
    db = o.shape[0]
    o5 = o.reshape(db, KV_HEADS, t * GROUP, KV_HEADS, HEAD_DIM)
    od = jnp.stack([o5[:, h, :, h, :] for h in range(KV_HEADS)], axis=1)
    return od.reshape(db, KV_HEADS, t, GROUP, HEAD_DIM).transpose(0, 2, 1, 3, 4).reshape(db, t, D_MODEL)


def _pad_new(x):
    return jnp.pad(x, ((0, 0), (0, NEW_PAD - x.shape[1]), (0, 0)))


def _row_token(shape):
    return (lax.broadcasted_iota(jnp.int32, shape, 0) >> 2) & 3


def _new_scores(qbd, knew):
    pad = jnp.zeros((NEW_ROWS - NEW_PAD, KV_DIM), BF16)
    s_new = _nt_dot(qbd, jnp.concatenate([knew.astype(BF16), pad], axis=0))
    tcol = lax.broadcasted_iota(jnp.int32, s_new.shape, 1)
    return s_new, tcol <= _row_token(s_new.shape)


def _new_token_part(e_new, v_new):
    out = e_new[:, 0:1] * v_new[0:1, :]
    for t in range(1, 4):
        out = out + e_new[:, t:t + 1] * v_new[t:t + 1, :]
    return out


def _moba_sample_kernel(pt_ref, qbd_ref, knew_ref, vnew_ref, ck_ref, cv_ref, o_ref,
                        buf, sem, s_all, p_all, gate_s, acc_s):
    i = pl.program_id(0)
    ns = pl.num_programs(0)
    n_pages = pt_ref.shape[1]
    k_items = n_pages // PAGES_PER_ITEM
    n_items = 2 * k_items
    item_cols = PAGES_PER_ITEM * PAGE_SIZE
    blocks_per_item = item_cols // MOBA_BLOCK
    n_blocks = k_items * blocks_per_item
    nb_pad = gate_s.shape[1]
    qbd = qbd_ref[0]

    def page_copy(src, page, slot, pg):
        return pltpu.make_async_copy(src.at[page], buf.at[slot, :, pl.ds(pg * PAGE_SIZE, PAGE_SIZE)],
                                     sem.at[slot])

    def start_item(sample, it, slot):
        src = ck_ref if it < k_items else cv_ref
        c = it % k_items
        for pg in range(PAGES_PER_ITEM):
            page_copy(src, pt_ref[sample, c * PAGES_PER_ITEM + pg], slot, pg).start()

    def wait_item(slot):
        for pg in range(PAGES_PER_ITEM):
            page_copy(ck_ref, 0, slot, pg).wait()

    @pl.when(i == 0)
    def _():
        start_item(0, 0, 0)

    gate_s[...] = jnp.zeros_like(gate_s)
    glane = lax.broadcasted_iota(jnp.int32, (SAMPLE_ROWS, nb_pad), 1)
    own_out = None
    l_tot = None
    for it in range(n_items):
        slot = it % 2
        if it + 1 < n_items:
            start_item(i, it + 1, 1 - slot)
        else:
            @pl.when(i + 1 < ns)
            def _():
                start_item(i + 1, 0, 1 - slot)
        wait_item(slot)

        if it < k_items:
            s = jnp.dot(qbd, buf[slot].astype(BF16), preferred_element_type=F32)
            s_all[:, it * item_cols:(it + 1) * item_cols] = s
            gate = gate_s[...]
            for bi in range(blocks_per_item):
                gate = jnp.where(glane == it * blocks_per_item + bi,
                                 jnp.sum(s[:, bi * MOBA_BLOCK:(bi + 1) * MOBA_BLOCK], axis=1, keepdims=True), gate)
            gate_s[...] = gate

        if it == k_items - 1:
            g = jnp.where(glane < n_blocks, gate_s[...], NEG_INF)
            sel = jnp.zeros((SAMPLE_ROWS, nb_pad), jnp.bool_)
            for _ in range(min(MOBA_TOPK, n_blocks)):
                mx = jnp.max(g, axis=1, keepdims=True)
                idx = jnp.min(jnp.where(g == mx, glane, nb_pad), axis=1, keepdims=True)
                hit = glane == idx
                sel = sel | hit
                g = jnp.where(hit, NEG_INF, g)
            sel_bf = sel.astype(BF16)

            s_new, new_ok = _new_scores(qbd, knew_ref[0])
            m = jnp.max(jnp.where(new_ok, s_new, NEG_INF), axis=1, keepdims=True)

            def expand(c):
                blk = lax.broadcasted_iota(jnp.int32, (nb_pad, item_cols), 0)
                key_blk = lax.broadcasted_iota(jnp.int32, (nb_pad, item_cols), 1) >> BLOCK_SHIFT
                ex = (blk == key_blk + c * blocks_per_item).astype(BF16)
                return jnp.dot(sel_bf, ex, preferred_element_type=F32) > 0.5

            for c in range(k_items):
                sc = jnp.where(expand(c), s_all[:, c * item_cols:(c + 1) * item_cols], NEG_INF)
                m = jnp.maximum(m, jnp.max(sc, axis=1, keepdims=True))
            e_new = jnp.where(new_ok, jnp.exp2(s_new - m), 0.0)
            l_tot = jnp.sum(e_new, axis=1, keepdims=True)
            for c in range(k_items):
                sc = jnp.where(expand(c), s_all[:, c * item_cols:(c + 1) * item_cols], NEG_INF)
                pc = jnp.exp2(sc - m)
                l_tot = l_tot + jnp.sum(pc, axis=1, keepdims=True)
                p_all[:, c * item_cols:(c + 1) * item_cols] = pc.astype(BF16)
            own_out = _new_token_part(e_new, vnew_ref[0])
            acc_s[...] = jnp.zeros_like(acc_s)

        if it >= k_items:
            c = it - k_items
            acc_s[...] += _nt_dot(p_all[:, c * item_cols:(c + 1) * item_cols], buf[slot].astype(BF16))

    o_ref[0] = (acc_s[...] + own_out) / l_tot


def _moba_sample(qbd, k_new, v_new, cache_kt, cache_vt, page_table):
    db = qbd.shape[0]
    n_pages = page_table.shape[1]
    past = n_pages * PAGE_SIZE
    item_cols = PAGES_PER_ITEM * PAGE_SIZE
    nb_pad = -(-(past // MOBA_BLOCK) // LANES) * LANES
    grid_spec = pltpu.PrefetchScalarGridSpec(
        num_scalar_prefetch=1,
        grid=(db,),
        in_specs=[pl.BlockSpec((1, SAMPLE_ROWS, KV_DIM), lambda i, pt: (i, 0, 0)),
                  pl.BlockSpec((1, NEW_PAD, KV_DIM), lambda i, pt: (i, 0, 0)),
                  pl.BlockSpec((1, NEW_PAD, KV_DIM), lambda i, pt: (i, 0, 0)),
                  pl.BlockSpec(memory_space=pl.ANY),
                  pl.BlockSpec(memory_space=pl.ANY)],
        out_specs=pl.BlockSpec((1, SAMPLE_ROWS, KV_DIM), lambda i, pt: (i, 0, 0)),
        scratch_shapes=[pltpu.VMEM((2, KV_DIM, item_cols), F32),
                        pltpu.SemaphoreType.DMA((2,)),
                        pltpu.VMEM((SAMPLE_ROWS, past), F32),
                        pltpu.VMEM((SAMPLE_ROWS, past), BF16),
                        pltpu.VMEM((SAMPLE_ROWS, nb_pad), F32),
                        pltpu.VMEM((SAMPLE_ROWS, KV_DIM), F32)],
    )
    return pl.pallas_call(
        _moba_sample_kernel,
        grid_spec=grid_spec,
        out_shape=jax.ShapeDtypeStruct((db, SAMPLE_ROWS, KV_DIM), F32),
        compiler_params=_params(("arbitrary",)),
        name="moba_sample",
    )(page_table, qbd, k_new, v_new, cache_kt, cache_vt)


def _swa_sample_kernel(qbd_ref, kbuf_ref, vbuf_ref, knew_ref, vnew_ref, sink_ref, o_ref):
    qbd = qbd_ref[0]
    wb = kbuf_ref.shape[1]
    s_buf = _nt_dot(qbd, kbuf_ref[0].astype(BF16))
    s_new, new_ok = _new_scores(qbd, knew_ref[0])
    bcol = lax.broadcasted_iota(jnp.int32, (SAMPLE_ROWS, wb), 1)
    bdiff = wb + _row_token((SAMPLE_ROWS, wb)) - bcol
    buf_ok = (bdiff >= 0) & (bdiff < WINDOW)
    sink = sink_ref[:, 0:1] * LOG2E
    m = jnp.maximum(jnp.max(jnp.where(buf_ok, s_buf, NEG_INF), axis=1, keepdims=True),
                    jnp.max(jnp.where(new_ok, s_new, NEG_INF), axis=1, keepdims=True))
    m = jnp.maximum(m, sink)
    e_buf = jnp.where(buf_ok, jnp.exp2(s_buf - m), 0.0)
    e_new = jnp.where(new_ok, jnp.exp2(s_new - m), 0.0)
    denom = jnp.sum(e_buf, axis=1, keepdims=True) + jnp.sum(e_new, axis=1, keepdims=True) + jnp.exp2(sink - m)
    out = jnp.dot(e_buf.astype(BF16), vbuf_ref[0].astype(BF16), preferred_element_type=F32)
    o_ref[0] = (out + _new_token_part(e_new, vnew_ref[0])) / denom


def _swa_sample(qbd, k_buf, v_buf, k_new, v_new, sink_rows):
    db, wb, _ = k_buf.shape
    return pl.pallas_call(
        _swa_sample_kernel,
        grid=(db,),
        in_specs=[pl.BlockSpec((1, SAMPLE_ROWS, KV_DIM), lambda i: (i, 0, 0)),
                  pl.BlockSpec((1, wb, KV_DIM), lambda i: (i, 0, 0)),
                  pl.BlockSpec((1, wb, KV_DIM), lambda i: (i, 0, 0)),
                  pl.BlockSpec((1, NEW_PAD, KV_DIM), lambda i: (i, 0, 0)),
                  pl.BlockSpec((1, NEW_PAD, KV_DIM), lambda i: (i, 0, 0)),
                  pl.BlockSpec((SAMPLE_ROWS, LANES), lambda i: (0, 0))],
        out_specs=pl.BlockSpec((1, SAMPLE_ROWS, KV_DIM), lambda i: (i, 0, 0)),
        out_shape=jax.ShapeDtypeStruct((db, SAMPLE_ROWS, KV_DIM), F32),
        compiler_params=_params(("parallel",)),
        name="swa_sample",
    )(qbd, k_buf, v_buf, k_new, v_new, sink_rows)


def _first_of(vals, m):
    idx = jnp.full(m.shape, EXPERTS_PER_GROUP - 1, jnp.int32)
    for k in range(EXPERTS_PER_GROUP - 2, -1, -1):
        idx = jnp.where(vals[k] == m, k, idx)
    return idx


def _route_t(aff, bias):
    biased = aff + bias
    best = e1 = e2 = None
    for grp in range(N_GROUPS):
        v = [biased[grp * EXPERTS_PER_GROUP + k:grp * EXPERTS_PER_GROUP + k + 1, :] for k in range(EXPERTS_PER_GROUP)]
        m1 = jnp.maximum(jnp.maximum(v[0], v[1]), jnp.maximum(v[2], v[3]))
        i1 = _first_of(v, m1)
        v2 = [jnp.where(i1 == k, NEG_INF, v[k]) for k in range(EXPERTS_PER_GROUP)]
        m2 = jnp.maximum(jnp.maximum(v2[0], v2[1]), jnp.maximum(v2[2], v2[3]))
        i2 = _first_of(v2, m2)
        gs = m1 + m2
        if grp == 0:
            best, e1, e2 = gs, i1, i2
        else:
            better = gs > best
            best = jnp.where(better, gs, best)
            e1 = jnp.where(better, grp * EXPERTS_PER_GROUP + i1, e1)
            e2 = jnp.where(better, grp * EXPERTS_PER_GROUP + i2, e2)
    eio = lax.broadcasted_iota(jnp.int32, aff.shape, 0)
    sel1 = eio == e1
    sel2 = eio == e2
    a1 = jnp.sum(jnp.where(sel1, aff, 0.0), axis=0, keepdims=True)
    a2 = jnp.sum(jnp.where(sel2, aff, 0.0), axis=0, keepdims=True)
    den = a1 + a2
    return jnp.where(sel1, a1 / den, 0.0) + jnp.where(sel2, a2 / den, 0.0)


def _post_attn_kernel(a_ref, x_ref, wo_ref, g_ref, b_ref, wr_ref, br_ref, x1_ref, gates_ref):
    o = jnp.dot(a_ref[...], wo_ref[...], preferred_element_type=F32)
    x1 = _layer_norm(DEEPNORM_ALPHA * x_ref[...] + o, g_ref[...], b_ref[...])
    x1_ref[...] = x1
    x_hi = x1.astype(BF16)
    x_lo = (x1 - x_hi.astype(F32)).astype(BF16)
    w2 = wr_ref[...]
    t_hi = _nt_dot(w2, x_hi)
    t_lo = _nt_dot(w2[:N_EXPERTS], x_lo)
    logits = t_hi[:N_EXPERTS] + t_hi[N_EXPERTS:] + t_lo
    gates_t = _route_t(jax.nn.sigmoid(logits), br_ref[...])
    tm = x1.shape[0]
    gates_ref[...] = jnp.concatenate([gates_t, jnp.zeros((LANES - N_EXPERTS, tm), F32)], axis=0).T


def _post_attn(attn, x2d, wo_bf, ln_g, ln_b, wr_split, br_col, *, tm):
    t = x2d.shape[0]
    return pl.pallas_call(
        _post_attn_kernel,
        grid=(t // tm,),
        in_specs=[pl.BlockSpec((tm, D_MODEL), lambda i: (i, 0)),
                  pl.BlockSpec((tm, D_MODEL), lambda i: (i, 0)),
                  pl.BlockSpec((D_MODEL, D_MODEL), lambda i: (0, 0)),
                  pl.BlockSpec((1, D_MODEL), lambda i: (0, 0)),
                  pl.BlockSpec((1, D_MODEL), lambda i: (0, 0)),
                  pl.BlockSpec((2 * N_EXPERTS, D_MODEL), lambda i: (0, 0)),
                  pl.BlockSpec((N_EXPERTS, 1), lambda i: (0, 0))],
        out_specs=[pl.BlockSpec((tm, D_MODEL), lambda i: (i, 0)),
                   pl.BlockSpec((tm, LANES), lambda i: (i, 0))],
        out_shape=[jax.ShapeDtypeStruct((t, D_MODEL), F32),
                   jax.ShapeDtypeStruct((t, LANES), F32)],
        compiler_params=_params(("parallel",)),
        name="post_attn",
    )(attn, x2d, wo_bf, ln_g, ln_b, wr_split, br_col)


def _moe_kernel(x1_ref, gates_ref, wg_ref, wu_ref, wd_ref, g_ref, b_ref, p_ref, wple_ref, wpg_ref,
                out_ref, xb_ref, acc_ref):
    e = pl.program_id(1)

    @pl.when(e == 0)
    def _():
        xb_ref[...] = x1_ref[...].astype(BF16)
        acc_ref[...] = jnp.zeros_like(acc_ref)

    xb = xb_ref[...]
    hg = jnp.dot(xb, wg_ref[0], preferred_element_type=F32)
    hu = jnp.dot(xb, wu_ref[0], preferred_element_type=F32)
    h = hg * jax.nn.sigmoid(hg) * hu
    y = jnp.dot(h.astype(BF16), wd_ref[0], preferred_element_type=F32)
    gates = gates_ref[...]
    lane = lax.broadcasted_iota(jnp.int32, gates.shape, 1)
    gate_e = jnp.sum(jnp.where(lane == e, gates, 0.0), axis=1, keepdims=True)
    acc_ref[...] += gate_e * y

    @pl.when(e == pl.num_programs(1) - 1)
    def _():
        x2 = _layer_norm(DEEPNORM_ALPHA * x1_ref[...] + acc_ref[...], g_ref[...], b_ref[...])
        ple = jnp.dot(p_ref[...].astype(BF16), wple_ref[...], preferred_element_type=F32)
        gt = jax.nn.sigmoid(jnp.dot(x2.astype(BF16), wpg_ref[...], preferred_element_type=F32))
        out_ref[...] = x2 + ple * gt


def _moe(x1, gates, wg_bf, wu_bf, wd_bf, ln_g, ln_b, p2d, wple_bf, wpg_bf, *, tm):
    t = x1.shape[0]
    return pl.pallas_call(
        _moe_kernel,
        grid=(t // tm, N_EXPERTS),
        in_specs=[pl.BlockSpec((tm, D_MODEL), lambda i, e: (i, 0)),
                  pl.BlockSpec((tm, LANES), lambda i, e: (i, 0)),
                  pl.BlockSpec((1, D_MODEL, D_EXPERT), lambda i, e: (e, 0, 0)),
                  pl.BlockSpec((1, D_MODEL, D_EXPERT), lambda i, e: (e, 0, 0)),
                  pl.BlockSpec((1, D_EXPERT, D_MODEL), lambda i, e: (e, 0, 0)),
                  pl.BlockSpec((1, D_MODEL), lambda i, e: (0, 0)),
                  pl.BlockSpec((1, D_MODEL), lambda i, e: (0, 0)),
                  pl.BlockSpec((tm, PLE_DIM), lambda i, e: (i, 0)),
                  pl.BlockSpec((PLE_DIM, D_MODEL), lambda i, e: (0, 0)),
                  pl.BlockSpec((D_MODEL, D_MODEL), lambda i, e: (0, 0))],
        out_specs=pl.BlockSpec((tm, D_MODEL), lambda i, e: (i, 0)),
        out_shape=jax.ShapeDtypeStruct((t, D_MODEL), F32),
        scratch_shapes=[pltpu.VMEM((tm, D_MODEL), BF16),
                        pltpu.VMEM((tm, D_MODEL), F32)],
        compiler_params=_params(("parallel", "arbitrary")),
        name="moe",
    )(x1, gates, wg_bf, wu_bf, wd_bf, ln_g, ln_b, p2d, wple_bf, wpg_bf)


def _token_tile(t):
    return 512 if t % 512 == 0 else t


def _ffn_block(attn2d, x2d, p2d, layer, wts, *, tm, tm_moe):
    x1, gates = _post_attn(attn2d, x2d, wts["w_o"][layer], wts["ln_g"][layer, 0:1], wts["ln_b"][layer, 0:1],
                           wts["w_router"], wts["b_router"], tm=tm)
    return _moe(x1, gates, wts["w_gate"][layer], wts["w_up"][layer], wts["w_down"][layer],
                wts["ln_g"][layer, 1:2], wts["ln_b"][layer, 1:2], p2d, wts["w_ple"][layer],
                wts["w_ple_gate"][layer], tm=tm_moe)


def _prompt_trunk(x, p, wts):
    b, s, _ = x.shape
    t = b * s
    tm = _token_tile(s)
    tm_moe = 1024 if t % 1024 == 0 else tm
    cs = _rope_tables(jnp.arange(s))
    x2d = x.reshape(t, D_MODEL)

    q, k0, v0, khm, vt, km = _proj_rope(x2d, wts["w_qkv"][0], cs, seq=s, tm=tm, attn_layout=True, with_kmean=True)
    kmean = km.reshape(b, s // MOBA_BLOCK, KV_HEADS, HEAD_DIM).transpose(0, 2, 1, 3)
    attn = _moba_prompt(q.reshape(b, s, D_MODEL), khm, vt, kmean)
    x2d = _ffn_block(attn.reshape(t, D_MODEL), x2d, p[0].reshape(t, PLE_DIM), 0, wts, tm=tm, tm_moe=tm_moe)

    q, k1, v1, khm, vt = _proj_rope(x2d, wts["w_qkv"][1], cs, seq=s, tm=tm, attn_layout=True, with_kmean=False)
    attn = _swa_prompt(q.reshape(b, s, D_MODEL), khm, vt, wts["sinks"])
    x2d = _ffn_block(attn.reshape(t, D_MODEL), x2d, p[1].reshape(t, PLE_DIM), 1, wts, tm=tm, tm_moe=tm_moe)
    return (x2d.reshape(b, s, D_MODEL), k0.reshape(b, s, KV_HEADS, HEAD_DIM), v0.reshape(b, s, KV_HEADS, HEAD_DIM),
            k1.reshape(b, s, KV_HEADS, HEAD_DIM), v1.reshape(b, s, KV_HEADS, HEAD_DIM))


def _sample_trunk(x, p, cache_k, cache_v, state_k, state_v, page_table, wts):
    db, tn, _ = x.shape
    t = db * tn
    tm = _token_tile(t)
    pos = PAST_LEN + jnp.arange(tn)
    cs = jnp.tile(_rope_tables(pos), (db, 1))
    x2d = x.reshape(t, D_MODEL)

    q, k0, v0 = _proj_rope(x2d, wts["w_qkv"][0], cs, seq=tn, tm=tm, attn_layout=False, with_kmean=False)
    o = _moba_sample(_block_diag_q(q.reshape(db, tn, D_MODEL)),
                     _pad_new(k0.reshape(db, tn, KV_DIM)), _pad_new(v0.reshape(db, tn, KV_DIM)),
                     _cache_pages_t(cache_k), _cache_pages_t(cache_v), page_table)
    attn = _block_diag_out(o, tn).astype(BF16)
    x2d = _ffn_block(attn.reshape(t, D_MODEL), x2d, p[0].reshape(t, PLE_DIM), 0, wts, tm=tm, tm_moe=tm)

    q, k1, v1 = _proj_rope(x2d, wts["w_qkv"][1], cs, seq=tn, tm=tm, attn_layout=False, with_kmean=False)
    wb = state_k.shape[1]
    sink_rows = jnp.broadcast_to(
        jnp.broadcast_to(wts["sinks"].reshape(KV_HEADS, 1, GROUP), (KV_HEADS, tn, GROUP)).reshape(SAMPLE_ROWS, 1),
        (SAMPLE_ROWS, LANES))
    o = _swa_sample(_block_diag_q(q.reshape(db, tn, D_MODEL)),
                    state_k.reshape(db, wb, KV_DIM), state_v.reshape(db, wb, KV_DIM),
                    _pad_new(k1.reshape(db, tn, KV_DIM)), _pad_new(v1.reshape(db, tn, KV_DIM)), sink_rows)
    attn = _block_diag_out(o, tn).astype(BF16)
    x2d = _ffn_block(attn.reshape(t, D_MODEL), x2d, p[1].reshape(t, PLE_DIM), 1, wts, tm=tm, tm_moe=tm)
    return (x2d.reshape(db, tn, D_MODEL), k0.reshape(db, tn, KV_HEADS, HEAD_DIM), v0.reshape(db, tn, KV_HEADS, HEAD_DIM),
            k1.reshape(db, tn, KV_HEADS, HEAD_DIM), v1.reshape(db, tn, KV_HEADS, HEAD_DIM))


def _split_hi_lo(w):
    hi = w.astype(BF16)
    lo = (w - hi.astype(F32)).astype(BF16)
    return jnp.concatenate([hi, lo], axis=0)


def _cache_pages_t(cache):
    pool, page = cache.shape[1], cache.shape[2]
    return jnp.transpose(cache.reshape(pool, page, KV_HEADS, HEAD_DIM), (0, 2, 3, 1)).reshape(pool, KV_DIM, page)


def kernel(x_prompt, x_sample, p_prompt, p_sample, cache_k_a, cache_v_a, state_swa_k, state_swa_v, page_table,
           w_qkv_a, w_o_a, w_kv_s, w_q_b, w_o_b, sinks_b, ln_g, ln_b, w_router, b_router,
           w_exp_gate, w_exp_up, w_exp_down, w_ple, w_ple_gate):
    wts = {
        "w_qkv": jnp.stack([w_qkv_a[0], jnp.concatenate([w_q_b[0], w_kv_s], axis=1)]).astype(BF16),
        "w_o": jnp.stack([w_o_a[0], w_o_b[0]]).astype(BF16),
        "sinks": sinks_b[0].astype(F32),
        "ln_g": ln_g, "ln_b": ln_b,
        "w_router": _split_hi_lo(w_router.astype(F32).T),
        "b_router": b_router.astype(F32).reshape(N_EXPERTS, 1),
        "w_gate": w_exp_gate.astype(BF16), "w_up": w_exp_up.astype(BF16), "w_down": w_exp_down.astype(BF16),
        "w_ple": w_ple.astype(BF16), "w_ple_gate": w_ple_gate.astype(BF16),
    }
    y_p, ka_p, va_p, ks_p, vs_p = _prompt_trunk(x_prompt, p_prompt, wts)
    y_s, ka_s, va_s, ks_s, vs_s = _sample_trunk(x_sample, p_sample, cache_k_a, cache_v_a,
                                                state_swa_k, state_swa_v, page_table, wts)
    wb_p = min(WINDOW, x_prompt.shape[1])
    tn = x_sample.shape[1]
    return (y_p, y_s, ka_p[None], va_p[None], ka_s[None], va_s[None],
            ks_p[:, -wb_p:], vs_p[:, -wb_p:],
            jnp.concatenate([state_swa_k, ks_s], axis=1)[:, tn:],
            jnp.concatenate([state_swa_v, vs_s], axis=1)[:, tn:])
```

```python
import functools

import jax
import jax.numpy as jnp
from jax import lax
from jax.experimental import pallas as pl
from jax.experimental.pallas import tpu as pltpu

D_MODEL = 1024
HEAD_DIM = 64
HALF_DIM = HEAD_DIM // 2
N_HEADS = D_MODEL // HEAD_DIM
KV_HEADS = 4
GROUP = N_HEADS // KV_HEADS
KV_DIM = KV_HEADS * HEAD_DIM
MOBA_BLOCK = 256
BLOCK_SHIFT = MOBA_BLOCK.bit_length() - 1
MOBA_TOPK = 3
WINDOW = 128
PAGE_SIZE = 128
PAST_LEN = 16384
ROPE_THETA = 10000.0
N_EXPERTS = 16
N_GROUPS = 4
EXPERTS_PER_GROUP = N_EXPERTS // N_GROUPS
D_EXPERT = D_MODEL // 2
PLE_DIM = 256
DEPTH = 2
DEEPNORM_ALPHA = (2.0 * DEPTH) ** 0.25
LN_EPS = 1e-5
ATTN_SCALE = HEAD_DIM ** -0.5
LOG2E = 1.4426950408889634
Q_SCALE = ATTN_SCALE * LOG2E

LANES = 128
Q_ROWS = GROUP * MOBA_BLOCK
SAMPLE_ROWS = KV_HEADS * GROUP * 4
NEW_PAD = 8
NEW_ROWS = 128
PAGES_PER_ITEM = 32
BLOCKS_PER_TRIP = 4
VMEM_LIMIT = 56 * 1024 * 1024

F32 = jnp.float32
BF16 = jnp.bfloat16
NEG_INF = float("-inf")
NT_DIMS = (((1,), (1,)), ((), ()))


def _nt_dot(a, b):
    return lax.dot_general(a, b, NT_DIMS, preferred_element_type=F32)


def _params(semantics):
    return pltpu.CompilerParams(dimension_semantics=semantics, vmem_limit_bytes=VMEM_LIMIT)


def _layer_norm(z, g, b):
    mu = jnp.mean(z, axis=-1, keepdims=True)
    zc = z - mu
    var = jnp.mean(zc * zc, axis=-1, keepdims=True)
    return zc * lax.rsqrt(var + LN_EPS) * g + b


def _rope_tables(pos):
    inv_freq = ROPE_THETA ** (-jnp.arange(HALF_DIM, dtype=F32) / HALF_DIM)
    ang = pos.astype(F32)[:, None] * inv_freq[None, :]
    cos = jnp.cos(ang)
    sin = jnp.sin(ang)
    cos_t = jnp.tile(cos, (1, 4))
    sin_t = jnp.tile(jnp.concatenate([-sin, sin], axis=1), (1, 2))
    return jnp.concatenate([cos_t, sin_t], axis=1)


def _proj_rope_kernel(x_ref, w_ref, cs_ref, q_ref, k_ref, v_ref, *rest, attn_layout, with_kmean):
    x = x_ref[...].astype(BF16)
    qkv = jnp.dot(x, w_ref[...], preferred_element_type=F32)
    tm = x.shape[0]
    cos = cs_ref[:, :LANES]
    sin = cs_ref[:, LANES:]
    lane = lax.broadcasted_iota(jnp.int32, (tm, LANES), 1)
    first_half = (lane & HALF_DIM) == 0

    def rope(blk):
        partner = jnp.where(first_half, pltpu.roll(blk, LANES - HALF_DIM, 1), pltpu.roll(blk, HALF_DIM, 1))
        return blk * cos + partner * sin

    for j in range(D_MODEL // LANES):
        q_ref[:, j * LANES:(j + 1) * LANES] = (rope(qkv[:, j * LANES:(j + 1) * LANES]) * Q_SCALE).astype(BF16)
    k = jnp.concatenate([rope(qkv[:, D_MODEL + j * LANES:D_MODEL + (j + 1) * LANES])
                         for j in range(KV_DIM // LANES)], axis=1)
    v = qkv[:, D_MODEL + KV_DIM:]
    k_ref[...] = k
    v_ref[...] = v
    if attn_layout:
        khm_ref, vt_ref = rest[0], rest[1]
        for h in range(KV_HEADS):
            khm_ref[0, h] = k[:, h * HEAD_DIM:(h + 1) * HEAD_DIM].astype(BF16)
        vt_ref[0] = v.T.astype(BF16)
    if with_kmean:
        km_ref = rest[2]
        nb = tm // MOBA_BLOCK
        km_ref[0] = jnp.sum(k.reshape(nb, MOBA_BLOCK, KV_DIM), axis=1) * (1.0 / MOBA_BLOCK)


def _proj_rope(x2d, w_bf, cs, *, seq, tm, attn_layout, with_kmean):
    t = x2d.shape[0]
    nt = t // tm
    tab_tiles = cs.shape[0] // tm
    tiles_per_seq = seq // tm if attn_layout else 1
    out_shape = [jax.ShapeDtypeStruct((t, D_MODEL), BF16),
                 jax.ShapeDtypeStruct((t, KV_DIM), F32),
                 jax.ShapeDtypeStruct((t, KV_DIM), F32)]
    out_specs = [pl.BlockSpec((tm, D_MODEL), lambda i: (i, 0)),
                 pl.BlockSpec((tm, KV_DIM), lambda i: (i, 0)),
                 pl.BlockSpec((tm, KV_DIM), lambda i: (i, 0))]
    if attn_layout:
        nseq = t // seq
        out_shape += [jax.ShapeDtypeStruct((nseq, KV_HEADS, seq, HEAD_DIM), BF16),
                      jax.ShapeDtypeStruct((nseq, KV_DIM, seq), BF16)]
        out_specs += [pl.BlockSpec((1, KV_HEADS, tm, HEAD_DIM),
                                   lambda i: (i // tiles_per_seq, 0, i % tiles_per_seq, 0)),
                      pl.BlockSpec((1, KV_DIM, tm), lambda i: (i // tiles_per_seq, 0, i % tiles_per_seq))]
    if with_kmean:
        out_shape.append(jax.ShapeDtypeStruct((nt, tm // MOBA_BLOCK, KV_DIM), F32))
        out_specs.append(pl.BlockSpec((1, tm // MOBA_BLOCK, KV_DIM), lambda i: (i, 0, 0)))
    return pl.pallas_call(
        functools.partial(_proj_rope_kernel, attn_layout=attn_layout, with_kmean=with_kmean),
        grid=(nt,),
        in_specs=[pl.BlockSpec((tm, D_MODEL), lambda i: (i, 0)),
                  pl.BlockSpec(w_bf.shape, lambda i: (0, 0)),
                  pl.BlockSpec((tm, 2 * LANES), lambda i: (i % tab_tiles, 0))],
        out_specs=out_specs,
        out_shape=out_shape,
        compiler_params=_params(("parallel",)),
        name="proj_rope",
    )(x2d, w_bf, cs)


def _stack_heads(qb):
    return jnp.concatenate([qb[:, g * HEAD_DIM:(g + 1) * HEAD_DIM] for g in range(GROUP)], axis=0)


def _unstack_out(out_t):
    o2 = jnp.concatenate([out_t[:, g * MOBA_BLOCK:(g + 1) * MOBA_BLOCK] for g in range(GROUP)], axis=0)
    return o2.T.astype(BF16)


def _lane_chunks():
    return [slice(c * LANES, (c + 1) * LANES) for c in range(Q_ROWS // LANES)]


def _moba_prompt_kernel(q_ref, k_ref, vt_ref, km_ref, o_ref, sel_ref, m_ref, l_ref, a_ref, p_ref, acc_ref):
    n = pl.program_id(2)
    nblk = km_ref.shape[2]
    qrows = _stack_heads(q_ref[0])

    gate = _nt_dot(km_ref[0, 0].astype(BF16), qrows)
    jio = lax.broadcasted_iota(jnp.int32, (nblk, Q_ROWS), 0)
    past = jio < n
    g = jnp.where(past, gate, NEG_INF)
    sel = jnp.zeros((nblk, Q_ROWS), jnp.bool_)
    for _ in range(MOBA_TOPK):
        mx = jnp.max(g, axis=0, keepdims=True)
        idx = jnp.min(jnp.where(g == mx, jio, nblk), axis=0, keepdims=True)
        hit = jio == idx
        sel = sel | hit
        g = jnp.where(hit, NEG_INF, g)
    sel_ref[...] = (sel & past).astype(F32)

    own = pl.multiple_of(n * MOBA_BLOCK, MOBA_BLOCK)
    k_own = k_ref[0, 0, pl.ds(own, MOBA_BLOCK), :]
    kio = lax.broadcasted_iota(jnp.int32, (MOBA_BLOCK, LANES), 0)
    lio = lax.broadcasted_iota(jnp.int32, (MOBA_BLOCK, LANES), 1)
    for ci, cs in enumerate(_lane_chunks()):
        s = _nt_dot(k_own, qrows[cs, :])
        qio = (lio + ci * LANES) & (MOBA_BLOCK - 1)
        s = jnp.where(kio <= qio, s, NEG_INF)
        m0 = jnp.max(s, axis=0, keepdims=True)
        p = jnp.exp2(s - m0)
        m_ref[:, cs] = m0
        l_ref[:, cs] = jnp.sum(p, axis=0, keepdims=True)
        p_ref[:, cs] = p.astype(BF16)
    acc_ref[...] = jnp.zeros_like(acc_ref)

    def pv(blk):
        off = pl.multiple_of(blk * MOBA_BLOCK, MOBA_BLOCK)
        return jnp.dot(vt_ref[0, :, pl.ds(off, MOBA_BLOCK)], p_ref[...], preferred_element_type=F32)

    def past_block(j):
        pv_prev = pv(jnp.where(j == 0, n, j - 1))
        off = pl.multiple_of(j * MOBA_BLOCK, MOBA_BLOCK)
        k_j = k_ref[0, 0, pl.ds(off, MOBA_BLOCK), :]
        sel_j = sel_ref[pl.ds(j, 1), :]
        for cs in _lane_chunks():
            sj = _nt_dot(k_j, qrows[cs, :])
            picked = sel_j[:, cs] > 0.0
            m_old = m_ref[:, cs]
            m_new = jnp.where(picked, jnp.maximum(m_old, jnp.max(sj, axis=0, keepdims=True)), m_old)
            a = jnp.exp2(m_old - m_new)
            pj = jnp.exp2(sj - jnp.where(picked, m_new, jnp.inf))
            l_ref[:, cs] = a * l_ref[:, cs] + jnp.sum(pj, axis=0, keepdims=True)
            m_ref[:, cs] = m_new
            a_ref[:, cs] = a
            p_ref[:, cs] = pj.astype(BF16)
        acc_ref[...] = a_ref[...] * (acc_ref[...] + pv_prev)

    rem = n & (BLOCKS_PER_TRIP - 1)

    def single(j, carry):
        past_block(j)
        return carry

    lax.fori_loop(0, rem, single, 0)

    def trip(i, carry):
        for u in range(BLOCKS_PER_TRIP):
            past_block(BLOCKS_PER_TRIP * i + rem + u)
        return carry

    lax.fori_loop(0, n >> (BLOCKS_PER_TRIP.bit_length() - 1), trip, 0)
    last = jnp.where(n == 0, 0, n - 1)
    o_ref[0] = _unstack_out((acc_ref[...] + pv(last)) / l_ref[...])


def _moba_prompt(q, khm, vt, kmean):
    b, s, _ = q.shape
    nblk = s // MOBA_BLOCK
    return pl.pallas_call(
        _moba_prompt_kernel,
        grid=(b, KV_HEADS, nblk),
        in_specs=[pl.BlockSpec((1, MOBA_BLOCK, KV_DIM), lambda bi, h, n: (bi, n, h)),
                  pl.BlockSpec((1, 1, s, HEAD_DIM), lambda bi, h, n: (bi, h, 0, 0)),
                  pl.BlockSpec((1, HEAD_DIM, s), lambda bi, h, n: (bi, h, 0)),
                  pl.BlockSpec((1, 1, nblk, HEAD_DIM), lambda bi, h, n: (bi, h, 0, 0))],
        out_specs=pl.BlockSpec((1, MOBA_BLOCK, KV_DIM), lambda bi, h, n: (bi, n, h)),
        out_shape=jax.ShapeDtypeStruct((b, s, D_MODEL), BF16),
        scratch_shapes=[pltpu.VMEM((nblk, Q_ROWS), F32),
                        pltpu.VMEM((1, Q_ROWS), F32),
                        pltpu.VMEM((1, Q_ROWS), F32),
                        pltpu.VMEM((1, Q_ROWS), F32),
                        pltpu.VMEM((MOBA_BLOCK, Q_ROWS), BF16),
                        pltpu.VMEM((HEAD_DIM, Q_ROWS), F32)],
        compiler_params=_params(("parallel", "parallel", "arbitrary")),
        name="moba_prompt",
    )(q, khm, vt, kmean)


SWA_KEYS = MOBA_BLOCK + WINDOW


def _swa_prompt_kernel(sink_ref, q_ref, k_ref, vt_ref, o_ref, d_ref, p_ref, bias_ref):
    h = pl.program_id(1)
    n = pl.program_id(2)
    qrows = _stack_heads(q_ref[0])
    start = pl.multiple_of(jnp.maximum(n * MOBA_BLOCK - WINDOW, 0), WINDOW)
    k_win = k_ref[0, 0, pl.ds(start, SWA_KEYS), :]

    @pl.when(n <= 1)
    def _():
        kpos = start + lax.broadcasted_iota(jnp.int32, (SWA_KEYS, LANES), 0)
        lio = lax.broadcasted_iota(jnp.int32, (SWA_KEYS, LANES), 1)
        for ci, cs in enumerate(_lane_chunks()):
            diff = n * MOBA_BLOCK + ((lio + ci * LANES) & (MOBA_BLOCK - 1)) - kpos
            bias_ref[:, cs] = jnp.where((diff >= 0) & (diff < WINDOW), 0.0, NEG_INF)

    for ci, cs in enumerate(_lane_chunks()):
        s = _nt_dot(k_win, qrows[cs, :]) + bias_ref[:, cs]
        sink = sink_ref[h * GROUP + (ci * LANES) // MOBA_BLOCK] * LOG2E
        m = jnp.maximum(jnp.max(s, axis=0, keepdims=True), sink)
        e = jnp.exp2(s - m)
        d_ref[:, cs] = jnp.sum(e, axis=0, keepdims=True) + jnp.exp2(sink - m)
        p_ref[:, cs] = e.astype(BF16)
    out_t = jnp.dot(vt_ref[0, :, pl.ds(start, SWA_KEYS)], p_ref[...], preferred_element_type=F32)
    o_ref[0] = _unstack_out(out_t / d_ref[...])


def _swa_prompt(q, khm, vt, sinks):
    b, s, _ = q.shape
    nblk = s // MOBA_BLOCK
    grid_spec = pltpu.PrefetchScalarGridSpec(
        num_scalar_prefetch=1,
        grid=(b, KV_HEADS, nblk),
        in_specs=[pl.BlockSpec((1, MOBA_BLOCK, KV_DIM), lambda bi, h, n, sk: (bi, n, h)),
                  pl.BlockSpec((1, 1, s, HEAD_DIM), lambda bi, h, n, sk: (bi, h, 0, 0)),
                  pl.BlockSpec((1, HEAD_DIM, s), lambda bi, h, n, sk: (bi, h, 0))],
        out_specs=pl.BlockSpec((1, MOBA_BLOCK, KV_DIM), lambda bi, h, n, sk: (bi, n, h)),
        scratch_shapes=[pltpu.VMEM((1, Q_ROWS), F32),
                        pltpu.VMEM((SWA_KEYS, Q_ROWS), BF16),
                        pltpu.VMEM((SWA_KEYS, Q_ROWS), F32)],
    )
    return pl.pallas_call(
        _swa_prompt_kernel,
        grid_spec=grid_spec,
        out_shape=jax.ShapeDtypeStruct((b, s, D_MODEL), BF16),
        compiler_params=_params(("parallel", "parallel", "arbitrary")),
        name="swa_prompt",
    )(sinks, q, khm, vt)


def _block_diag_q(q):
    db, t, _ = q.shape
    q5 = q.reshape(db, t, KV_HEADS, GROUP, HEAD_DIM).transpose(0, 2, 1, 3, 4)
    q5 = q5.reshape(db, KV_HEADS, t * GROUP, HEAD_DIM)
    eye = jnp.eye(KV_HEADS, dtype=q.dtype)
    return jnp.einsum("ikrd,kc->ikrcd", q5, eye).reshape(db, KV_HEADS * t * GROUP, KV_DIM)


def _block_diag_out(o, t):
    db = o.shape[0]
    o5 = o.reshape(db, KV_HEADS, t * GROUP, KV_HEADS, HEAD_DIM)
    od = jnp.stack([o5[:, h, :, h, :] for h in range(KV_HEADS)], axis=1)
    return od.reshape(db, KV_HEADS, t, GROUP, HEAD_DIM).transpose(0, 2, 1, 3, 4).reshape(db, t, D_MODEL)


def _pad_new(x):
    return jnp.pad(x, ((0, 0), (0, NEW_PAD - x.shape[1]), (0, 0)))


def _row_token(shape):
    return (lax.broadcasted_iota(jnp.int32, shape, 0) >> 2) & 3


def _new_scores(qbd, knew):
    pad = jnp.zeros((NEW_ROWS - NEW_PAD, KV_DIM), BF16)
    s_new = _nt_dot(qbd, jnp.concatenate([knew.astype(BF16), pad], axis=0))
    tcol = lax.broadcasted_iota(jnp.int32, s_new.shape, 1)
    return s_new, tcol <= _row_token(s_new.shape)


def _new_token_part(e_new, v_new):
    out = e_new[:, 0:1] * v_new[0:1, :]
    for t in range(1, 4):
        out = out + e_new[:, t:t + 1] * v_new[t:t + 1, :]
    return out


def _moba_sample_kernel(pt_ref, qbd_ref, knew_ref, vnew_ref, ck_ref, cv_ref, o_ref,
                        buf, sem, s_all, p_all, gate_s, acc_s):
    i = pl.program_id(0)
    ns = pl.num_programs(0)
    n_pages = pt_ref.shape[1]
    k_items = n_pages // PAGES_PER_ITEM
    n_items = 2 * k_items
    item_cols = PAGES_PER_ITEM * PAGE_SIZE
    blocks_per_item = item_cols // MOBA_BLOCK
    n_blocks = k_items * blocks_per_item
    nb_pad = gate_s.shape[1]
    qbd = qbd_ref[0]

    def page_copy(src, page, slot, pg):
        return pltpu.make_async_copy(src.at[page], buf.at[slot, :, pl.ds(pg * PAGE_SIZE, PAGE_SIZE)],
                                     sem.at[slot])

    def start_item(sample, it, slot):
        src = ck_ref if it < k_items else cv_ref
        c = it % k_items
        for pg in range(PAGES_PER_ITEM):
            page_copy(src, pt_ref[sample, c * PAGES_PER_ITEM + pg], slot, pg).start()

    def wait_item(slot):
        for pg in range(PAGES_PER_ITEM):
            page_copy(ck_ref, 0, slot, pg).wait()

    @pl.when(i == 0)
    def _():
        start_item(0, 0, 0)

    gate_s[...] = jnp.zeros_like(gate_s)
    glane = lax.broadcasted_iota(jnp.int32, (SAMPLE_ROWS, nb_pad), 1)
    own_out = None
    l_tot = None
    for it in range(n_items):
        slot = it % 2
        if it + 1 < n_items:
            start_item(i, it + 1, 1 - slot)
        else:
            @pl.when(i + 1 < ns)
            def _():
                start_item(i + 1, 0, 1 - slot)
        wait_item(slot)

        if it < k_items:
            s = jnp.dot(qbd, buf[slot].astype(BF16), preferred_element_type=F32)
            s_all[:, it * item_cols:(it + 1) * item_cols] = s
            gate = gate_s[...]
            for bi in range(blocks_per_item):
                gate = jnp.where(glane == it * blocks_per_item + bi,
                                 jnp.sum(s[:, bi * MOBA_BLOCK:(bi + 1) * MOBA_BLOCK], axis=1, keepdims=True), gate)
            gate_s[...] = gate

        if it == k_items - 1:
            g = jnp.where(glane < n_blocks, gate_s[...], NEG_INF)
            sel = jnp.zeros((SAMPLE_ROWS, nb_pad), jnp.bool_)
            for _ in range(min(MOBA_TOPK, n_blocks)):
                mx = jnp.max(g, axis=1, keepdims=True)
                idx = jnp.min(jnp.where(g == mx, glane, nb_pad), axis=1, keepdims=True)
                hit = glane == idx
                sel = sel | hit
                g = jnp.where(hit, NEG_INF, g)
            sel_bf = sel.astype(BF16)

            s_new, new_ok = _new_scores(qbd, knew_ref[0])
            m = jnp.max(jnp.where(new_ok, s_new, NEG_INF), axis=1, keepdims=True)

            def expand(c):
                blk = lax.broadcasted_iota(jnp.int32, (nb_pad, item_cols), 0)
                key_blk = lax.broadcasted_iota(jnp.int32, (nb_pad, item_cols), 1) >> BLOCK_SHIFT
                ex = (blk == key_blk + c * blocks_per_item).astype(BF16)
                return jnp.dot(sel_bf, ex, preferred_element_type=F32) > 0.5

            for c in range(k_items):
                sc = jnp.where(expand(c), s_all[:, c * item_cols:(c + 1) * item_cols], NEG_INF)
                m = jnp.maximum(m, jnp.max(sc, axis=1, keepdims=True))
            e_new = jnp.where(new_ok, jnp.exp2(s_new - m), 0.0)
            l_tot = jnp.sum(e_new, axis=1, keepdims=True)
            for c in range(k_items):
                sc = jnp.where(expand(c), s_all[:, c * item_cols:(c + 1) * item_cols], NEG_INF)
                pc = jnp.exp2(sc - m)
                l_tot = l_tot + jnp.sum(pc, axis=1, keepdims=True)
                p_all[:, c * item_cols:(c + 1) * item_cols] = pc.astype(BF16)
            own_out = _new_token_part(e_new, vnew_ref[0])
            acc_s[...] = jnp.zeros_like(acc_s)

        if it >= k_items:
            c = it - k_items
            acc_s[...] += _nt_dot(p_all[:, c * item_cols:(c + 1) * item_cols], buf[slot].astype(BF16))

    o_ref[0] = (acc_s[...] + own_out) / l_tot


def _moba_sample(qbd, k_new, v_new, cache_kt, cache_vt, page_table):
    db = qbd.shape[0]
    n_pages = page_table.shape[1]
    past = n_pages * PAGE_SIZE
    item_cols = PAGES_PER_ITEM * PAGE_SIZE
    nb_pad = -(-(past // MOBA_BLOCK) // LANES) * LANES
    grid_spec = pltpu.PrefetchScalarGridSpec(
        num_scalar_prefetch=1,
        grid=(db,),
        in_specs=[pl.BlockSpec((1, SAMPLE_ROWS, KV_DIM), lambda i, pt: (i, 0, 0)),
                  pl.BlockSpec((1, NEW_PAD, KV_DIM), lambda i, pt: (i, 0, 0)),
                  pl.BlockSpec((1, NEW_PAD, KV_DIM), lambda i, pt: (i, 0, 0)),
                  pl.BlockSpec(memory_space=pl.ANY),
                  pl.BlockSpec(memory_space=pl.ANY)],
        out_specs=pl.BlockSpec((1, SAMPLE_ROWS, KV_DIM), lambda i, pt: (i, 0, 0)),
        scratch_shapes=[pltpu.VMEM((2, KV_DIM, item_cols), F32),
                        pltpu.SemaphoreType.DMA((2,)),
                        pltpu.VMEM((SAMPLE_ROWS, past), F32),
                        pltpu.VMEM((SAMPLE_ROWS, past), BF16),
                        pltpu.VMEM((SAMPLE_ROWS, nb_pad), F32),
                        pltpu.VMEM((SAMPLE_ROWS, KV_DIM), F32)],
    )
    return pl.pallas_call(
        _moba_sample_kernel,
        grid_spec=grid_spec,
        out_shape=jax.ShapeDtypeStruct((db, SAMPLE_ROWS, KV_DIM), F32),
        compiler_params=_params(("arbitrary",)),
        name="moba_sample",
    )(page_table, qbd, k_new, v_new, cache_kt, cache_vt)


def _swa_sample_kernel(qbd_ref, kbuf_ref, vbuf_ref, knew_ref, vnew_ref, sink_ref, o_ref):
    qbd = qbd_ref[0]
    wb = kbuf_ref.shape[1]
    s_buf = _nt_dot(qbd, kbuf_ref[0].astype(BF16))
    s_new, new_ok = _new_scores(qbd, knew_ref[0])
    bcol = lax.broadcasted_iota(jnp.int32, (SAMPLE_ROWS, wb), 1)
    bdiff = wb + _row_token((SAMPLE_ROWS, wb)) - bcol
    buf_ok = (bdiff >= 0) & (bdiff < WINDOW)
    sink = sink_ref[:, 0:1] * LOG2E
    m = jnp.maximum(jnp.max(jnp.where(buf_ok, s_buf, NEG_INF), axis=1, keepdims=True),
                    jnp.max(jnp.where(new_ok, s_new, NEG_INF), axis=1, keepdims=True))
    m = jnp.maximum(m, sink)
    e_buf = jnp.where(buf_ok, jnp.exp2(s_buf - m), 0.0)
    e_new = jnp.where(new_ok, jnp.exp2(s_new - m), 0.0)
    denom = jnp.sum(e_buf, axis=1, keepdims=True) + jnp.sum(e_new, axis=1, keepdims=True) + jnp.exp2(sink - m)
    out = jnp.dot(e_buf.astype(BF16), vbuf_ref[0].astype(BF16), preferred_element_type=F32)
    o_ref[0] = (out + _new_token_part(e_new, vnew_ref[0])) / denom


def _swa_sample(qbd, k_buf, v_buf, k_new, v_new, sink_rows):
    db, wb, _ = k_buf.shape
    return pl.pallas_call(
        _swa_sample_kernel,
        grid=(db,),
        in_specs=[pl.BlockSpec((1, SAMPLE_ROWS, KV_DIM), lambda i: (i, 0, 0)),
                  pl.BlockSpec((1, wb, KV_DIM), lambda i: (i, 0, 0)),
                  pl.BlockSpec((1, wb, KV_DIM), lambda i: (i, 0, 0)),
                  pl.BlockSpec((1, NEW_PAD, KV_DIM), lambda i: (i, 0, 0)),
                  pl.BlockSpec((1, NEW_PAD, KV_DIM), lambda i: (i, 0, 0)),
                  pl.BlockSpec((SAMPLE_ROWS, LANES), lambda i: (0, 0))],
        out_specs=pl.BlockSpec((1, SAMPLE_ROWS, KV_DIM), lambda i: (i, 0, 0)),
        out_shape=jax.ShapeDtypeStruct((db, SAMPLE_ROWS, KV_DIM), F32),
        compiler_params=_params(("parallel",)),
        name="swa_sample",
    )(qbd, k_buf, v_buf, k_new, v_new, sink_rows)


def _first_of(vals, m):
    idx = jnp.full(m.shape, EXPERTS_PER_GROUP - 1, jnp.int32)
    for k in range(EXPERTS_PER_GROUP - 2, -1, -1):
        idx = jnp.where(vals[k] == m, k, idx)
    return idx


def _route_picks(aff, bias):
    biased = aff + bias
    best = e1 = e2 = None
    for grp in range(N_GROUPS):
        v = [biased[grp * EXPERTS_PER_GROUP + k:grp * EXPERTS_PER_GROUP + k + 1, :] for k in range(EXPERTS_PER_GROUP)]
        m1 = jnp.maximum(jnp.maximum(v[0], v[1]), jnp.maximum(v[2], v[3]))
        i1 = _first_of(v, m1)
        v2 = [jnp.where(i1 == k, NEG_INF, v[k]) for k in range(EXPERTS_PER_GROUP)]
        m2 = jnp.maximum(jnp.maximum(v2[0], v2[1]), jnp.maximum(v2[2], v2[3]))
        i2 = _first_of(v2, m2)
        gs = m1 + m2
        if grp == 0:
            best, e1, e2 = gs, i1, i2
        else:
            better = gs > best
            best = jnp.where(better, gs, best)
            e1 = jnp.where(better, grp * EXPERTS_PER_GROUP + i1, e1)
            e2 = jnp.where(better, grp * EXPERTS_PER_GROUP + i2, e2)
    eio = lax.broadcasted_iota(jnp.int32, aff.shape, 0)
    sel1 = eio == e1
    sel2 = eio == e2
    a1 = jnp.sum(jnp.where(sel1, aff, 0.0), axis=0, keepdims=True)
    a2 = jnp.sum(jnp.where(sel2, aff, 0.0), axis=0, keepdims=True)
    den = a1 + a2
    return e1, e2, a1 / den, a2 / den


def _route_t(aff, bias):
    e1, e2, g1, g2 = _route_picks(aff, bias)
    eio = lax.broadcasted_iota(jnp.int32, aff.shape, 0)
    return jnp.where(eio == e1, g1, 0.0) + jnp.where(eio == e2, g2, 0.0)


def _router_affinity(x1, wr_ref):
    x_hi = x1.astype(BF16)
    x_lo = (x1 - x_hi.astype(F32)).astype(BF16)
    w2 = wr_ref[...]
    t_hi = _nt_dot(w2, x_hi)
    t_lo = _nt_dot(w2[:N_EXPERTS], x_lo)
    return jax.nn.sigmoid(t_hi[:N_EXPERTS] + t_hi[N_EXPERTS:] + t_lo)


def _post_attn_kernel(a_ref, x_ref, wo_ref, g_ref, b_ref, wr_ref, br_ref, x1_ref, gates_ref):
    o = jnp.dot(a_ref[...], wo_ref[...], preferred_element_type=F32)
    x1 = _layer_norm(DEEPNORM_ALPHA * x_ref[...] + o, g_ref[...], b_ref[...])
    x1_ref[...] = x1
    gates_t = _route_t(_router_affinity(x1, wr_ref), br_ref[...])
    tm = x1.shape[0]
    gates_ref[...] = jnp.concatenate([gates_t, jnp.zeros((LANES - N_EXPERTS, tm), F32)], axis=0).T


def _post_attn(attn, x2d, wo_bf, ln_g, ln_b, wr_split, br_col, *, tm):
    t = x2d.shape[0]
    return pl.pallas_call(
        _post_attn_kernel,
        grid=(t // tm,),
        in_specs=[pl.BlockSpec((tm, D_MODEL), lambda i: (i, 0)),
                  pl.BlockSpec((tm, D_MODEL), lambda i: (i, 0)),
                  pl.BlockSpec((D_MODEL, D_MODEL), lambda i: (0, 0)),
                  pl.BlockSpec((1, D_MODEL), lambda i: (0, 0)),
                  pl.BlockSpec((1, D_MODEL), lambda i: (0, 0)),
                  pl.BlockSpec((2 * N_EXPERTS, D_MODEL), lambda i: (0, 0)),
                  pl.BlockSpec((N_EXPERTS, 1), lambda i: (0, 0))],
        out_specs=[pl.BlockSpec((tm, D_MODEL), lambda i: (i, 0)),
                   pl.BlockSpec((tm, LANES), lambda i: (i, 0))],
        out_shape=[jax.ShapeDtypeStruct((t, D_MODEL), F32),
                   jax.ShapeDtypeStruct((t, LANES), F32)],
        compiler_params=_params(("parallel",)),
        name="post_attn",
    )(attn, x2d, wo_bf, ln_g, ln_b, wr_split, br_col)


def _moe_kernel(x1_ref, gates_ref, wg_ref, wu_ref, wd_ref, g_ref, b_ref, p_ref, wple_ref, wpg_ref,
                out_ref, xb_ref, acc_ref):
    e = pl.program_id(1)

    @pl.when(e == 0)
    def _():
        xb_ref[...] = x1_ref[...].astype(BF16)
        acc_ref[...] = jnp.zeros_like(acc_ref)

    xb = xb_ref[...]
    hg = _nt_dot(xb, wg_ref[0])
    hu = _nt_dot(xb, wu_ref[0])
    h = hg * jax.nn.sigmoid(hg) * hu
    y = _nt_dot(h.astype(BF16), wd_ref[0])
    gates = gates_ref[...]
    lane = lax.broadcasted_iota(jnp.int32, gates.shape, 1)
    gate_e = jnp.sum(jnp.where(lane == e, gates, 0.0), axis=1, keepdims=True)
    acc_ref[...] += gate_e * y

    @pl.when(e == pl.num_programs(1) - 1)
    def _():
        x2 = _layer_norm(DEEPNORM_ALPHA * x1_ref[...] + acc_ref[...], g_ref[...], b_ref[...])
        ple = jnp.dot(p_ref[...].astype(BF16), wple_ref[...], preferred_element_type=F32)
        gt = jax.nn.sigmoid(jnp.dot(x2.astype(BF16), wpg_ref[...], preferred_element_type=F32))
        out_ref[...] = x2 + ple * gt


def _moe(x1, gates, wg_bf, wu_bf, wd_bf, ln_g, ln_b, p2d, wple_bf, wpg_bf, *, tm):
    t = x1.shape[0]
    return pl.pallas_call(
        _moe_kernel,
        grid=(t // tm, N_EXPERTS),
        in_specs=[pl.BlockSpec((tm, D_MODEL), lambda i, e: (i, 0)),
                  pl.BlockSpec((tm, LANES), lambda i, e: (i, 0)),
                  pl.BlockSpec((1, D_EXPERT, D_MODEL), lambda i, e: (e, 0, 0)),
                  pl.BlockSpec((1, D_EXPERT, D_MODEL), lambda i, e: (e, 0, 0)),
                  pl.BlockSpec((1, D_MODEL, D_EXPERT), lambda i, e: (e, 0, 0)),
                  pl.BlockSpec((1, D_MODEL), lambda i, e: (0, 0)),
                  pl.BlockSpec((1, D_MODEL), lambda i, e: (0, 0)),
                  pl.BlockSpec((tm, PLE_DIM), lambda i, e: (i, 0)),
                  pl.BlockSpec((PLE_DIM, D_MODEL), lambda i, e: (0, 0)),
                  pl.BlockSpec((D_MODEL, D_MODEL), lambda i, e: (0, 0))],
        out_specs=pl.BlockSpec((tm, D_MODEL), lambda i, e: (i, 0)),
        out_shape=jax.ShapeDtypeStruct((t, D_MODEL), F32),
        scratch_shapes=[pltpu.VMEM((tm, D_MODEL), BF16),
                        pltpu.VMEM((tm, D_MODEL), F32)],
        compiler_params=_params(("parallel", "arbitrary")),
        name="moe",
    )(x1, gates, wg_bf, wu_bf, wd_bf, ln_g, ln_b, p2d, wple_bf, wpg_bf)


MOE_TILE = 1024
CHUNK = LANES
META_ROWS = 8


def _sorted_rows(tm):
    return tm + N_GROUPS * CHUNK


def _one_hot_rows(pos_row, n_rows):
    tm = pos_row.shape[1]
    jio = lax.broadcasted_iota(jnp.int32, (n_rows, tm), 0)
    return (jio == pos_row.astype(jnp.int32)).astype(BF16)


def _dispatch_kernel(a_ref, x_ref, wo_ref, g_ref, b_ref, wr_ref, br_ref, tri_ref,
                     x1_ref, xs_ref, gs_ref, pos_ref, meta_ref):
    o = jnp.dot(a_ref[...], wo_ref[...], preferred_element_type=F32)
    x1 = _layer_norm(DEEPNORM_ALPHA * x_ref[...] + o, g_ref[...], b_ref[...])
    x1_ref[...] = x1
    tm = x1.shape[0]
    n_rows = xs_ref.shape[2]
    e1, e2, g1, g2 = _route_picks(_router_affinity(x1, wr_ref), br_ref[...])
    gid = e1 >> 2
    k1 = e1 & (EXPERTS_PER_GROUP - 1)
    k2 = e2 & (EXPERTS_PER_GROUP - 1)

    gio = lax.broadcasted_iota(jnp.int32, (META_ROWS, tm), 0)
    oh = gio == gid
    ohf = oh.astype(F32)
    rank = jnp.dot(oh.astype(BF16), tri_ref[...], preferred_element_type=F32)
    count = jnp.sum(ohf, axis=1, keepdims=True).astype(jnp.int32)
    nch = (count + (CHUNK - 1)) >> (CHUNK.bit_length() - 1)
    starts = [jnp.zeros((1, 1), jnp.int32)]
    for grp in range(1, N_GROUPS):
        starts.append(starts[-1] + nch[grp - 1:grp, :])
    pos = jnp.zeros((1, tm), F32)
    for grp in range(N_GROUPS):
        pos = pos + ohf[grp:grp + 1, :] * (rank[grp:grp + 1, :] + (starts[grp] * CHUNK).astype(F32))
    pos_ref[0] = jnp.broadcast_to(pos, (META_ROWS, tm))
    meta_ref[0] = jnp.concatenate(
        [jnp.broadcast_to(starts[grp], (1, LANES)) for grp in range(N_GROUPS)]
        + [jnp.broadcast_to(nch[grp:grp + 1, :], (1, LANES)) for grp in range(N_GROUPS)], axis=0)

    p_mat = _one_hot_rows(pos, n_rows)
    xs_ref[0] = _nt_dot(x1.T.astype(BF16), p_mat).astype(BF16)

    kio = lax.broadcasted_iota(jnp.int32, (META_ROWS, tm), 0)
    gk = jnp.where(kio == k1, g1, 0.0) + jnp.where(kio == k2, g2, 0.0)
    p1 = gk.astype(BF16).astype(F32)
    r1 = gk - p1
    p2 = r1.astype(BF16).astype(F32)
    p3 = r1 - p2
    pieces = _nt_dot(jnp.concatenate([p1, p2, p3], axis=0).astype(BF16), p_mat)
    gs_ref[0] = pieces[0:META_ROWS] + pieces[META_ROWS:2 * META_ROWS] + pieces[2 * META_ROWS:]


def _dispatch(attn, x2d, wo_bf, ln_g, ln_b, wr_split, br_col, tri, *, tm):
    t = x2d.shape[0]
    nt = t // tm
    n_rows = _sorted_rows(tm)
    return pl.pallas_call(
        _dispatch_kernel,
        grid=(nt,),
        in_specs=[pl.BlockSpec((tm, D_MODEL), lambda i: (i, 0)),
                  pl.BlockSpec((tm, D_MODEL), lambda i: (i, 0)),
                  pl.BlockSpec((D_MODEL, D_MODEL), lambda i: (0, 0)),
                  pl.BlockSpec((1, D_MODEL), lambda i: (0, 0)),
                  pl.BlockSpec((1, D_MODEL), lambda i: (0, 0)),
                  pl.BlockSpec((2 * N_EXPERTS, D_MODEL), lambda i: (0, 0)),
                  pl.BlockSpec((N_EXPERTS, 1), lambda i: (0, 0)),
                  pl.BlockSpec((tm, tm), lambda i: (0, 0))],
        out_specs=[pl.BlockSpec((tm, D_MODEL), lambda i: (i, 0)),
                   pl.BlockSpec((1, D_MODEL, n_rows), lambda i: (i, 0, 0)),
                   pl.BlockSpec((1, META_ROWS, n_rows), lambda i: (i, 0, 0)),
                   pl.BlockSpec((1, META_ROWS, tm), lambda i: (i, 0, 0)),
                   pl.BlockSpec((1, META_ROWS, LANES), lambda i: (i, 0, 0))],
        out_shape=[jax.ShapeDtypeStruct((t, D_MODEL), F32),
                   jax.ShapeDtypeStruct((nt, D_MODEL, n_rows), BF16),
                   jax.ShapeDtypeStruct((nt, META_ROWS, n_rows), F32),
                   jax.ShapeDtypeStruct((nt, META_ROWS, tm), F32),
                   jax.ShapeDtypeStruct((nt, META_ROWS, LANES), jnp.int32)],
        compiler_params=_params(("parallel",)),
        name="dispatch",
    )(attn, x2d, wo_bf, ln_g, ln_b, wr_split, br_col, tri)


def _experts_kernel(meta_ref, xs_ref, gs_ref, wg_ref, wu_ref, wd_ref, ys_ref):
    i = pl.program_id(0)
    grp = pl.program_id(1)

    @pl.when(grp == 0)
    def _():
        ys_ref[...] = jnp.zeros_like(ys_ref)

    first = meta_ref[i * META_ROWS + grp]
    n_chunks = meta_ref[i * META_ROWS + N_GROUPS + grp]

    def run(c, width):
        col = pl.multiple_of((first + c) * CHUNK, CHUNK)
        xc = xs_ref[0, :, pl.ds(col, width)]
        gates = gs_ref[0, :, pl.ds(col, width)]
        acc = jnp.zeros((D_MODEL, width), F32)
        for k in range(EXPERTS_PER_GROUP):
            hg = jnp.dot(wg_ref[k], xc, preferred_element_type=F32)
            hu = jnp.dot(wu_ref[k], xc, preferred_element_type=F32)
            h = hg * jax.nn.sigmoid(hg) * hu
            y = jnp.dot(wd_ref[k], h.astype(BF16), preferred_element_type=F32)
            acc = acc + gates[k:k + 1, :] * y
        ys_ref[0, :, pl.ds(col, width)] = acc.astype(BF16)

    def chunk_pair(c2, carry):
        run(2 * c2, 2 * CHUNK)
        return carry

    lax.fori_loop(0, n_chunks >> 1, chunk_pair, 0)

    @pl.when((n_chunks & 1) == 1)
    def _():
        run(n_chunks - 1, CHUNK)


def _experts(meta, xs, gs, wg_t, wu_t, wd_t):
    nt, _, n_rows = xs.shape
    grid_spec = pltpu.PrefetchScalarGridSpec(
        num_scalar_prefetch=1,
        grid=(nt, N_GROUPS),
        in_specs=[pl.BlockSpec((1, D_MODEL, n_rows), lambda i, g, m: (i, 0, 0)),
                  pl.BlockSpec((1, META_ROWS, n_rows), lambda i, g, m: (i, 0, 0)),
                  pl.BlockSpec((EXPERTS_PER_GROUP, D_EXPERT, D_MODEL), lambda i, g, m: (g, 0, 0)),
                  pl.BlockSpec((EXPERTS_PER_GROUP, D_EXPERT, D_MODEL), lambda i, g, m: (g, 0, 0)),
                  pl.BlockSpec((EXPERTS_PER_GROUP, D_MODEL, D_EXPERT), lambda i, g, m: (g, 0, 0))],
        out_specs=pl.BlockSpec((1, D_MODEL, n_rows), lambda i, g, m: (i, 0, 0)),
    )
    return pl.pallas_call(
        _experts_kernel,
        grid_spec=grid_spec,
        out_shape=jax.ShapeDtypeStruct((nt, D_MODEL, n_rows), BF16),
        compiler_params=_params(("parallel", "arbitrary")),
        name="experts",
    )(meta, xs, gs, wg_t, wu_t, wd_t)


def _combine_kernel(ys_ref, pos_ref, x1_ref, g_ref, b_ref, p_ref, wple_ref, wpg_ref, out_ref):
    n_rows = ys_ref.shape[2]
    p_mat = _one_hot_rows(pos_ref[0, 0:1, :], n_rows)
    ffn_t = jnp.dot(ys_ref[0], p_mat, preferred_element_type=F32)
    x2 = _layer_norm(DEEPNORM_ALPHA * x1_ref[...] + ffn_t.T, g_ref[...], b_ref[...])
    ple = jnp.dot(p_ref[...].astype(BF16), wple_ref[...], preferred_element_type=F32)
    gt = jax.nn.sigmoid(jnp.dot(x2.astype(BF16), wpg_ref[...], preferred_element_type=F32))
    out_ref[...] = x2 + ple * gt


def _combine(ys, pos, x1, ln_g, ln_b, p2d, wple_bf, wpg_bf, *, tm):
    t = x1.shape[0]
    n_rows = ys.shape[2]
    return pl.pallas_call(
        _combine_kernel,
        grid=(t // tm,),
        in_specs=[pl.BlockSpec((1, D_MODEL, n_rows), lambda i: (i, 0, 0)),
                  pl.BlockSpec((1, META_ROWS, tm), lambda i: (i, 0, 0)),
                  pl.BlockSpec((tm, D_MODEL), lambda i: (i, 0)),
                  pl.BlockSpec((1, D_MODEL), lambda i: (0, 0)),
                  pl.BlockSpec((1, D_MODEL), lambda i: (0, 0)),
                  pl.BlockSpec((tm, PLE_DIM), lambda i: (i, 0)),
                  pl.BlockSpec((PLE_DIM, D_MODEL), lambda i: (0, 0)),
                  pl.BlockSpec((D_MODEL, D_MODEL), lambda i: (0, 0))],
        out_specs=pl.BlockSpec((tm, D_MODEL), lambda i: (i, 0)),
        out_shape=jax.ShapeDtypeStruct((t, D_MODEL), F32),
        compiler_params=_params(("parallel",)),
        name="combine",
    )(ys, pos, x1, ln_g, ln_b, p2d, wple_bf, wpg_bf)


def _routed_ffn_block(attn2d, x2d, p2d, layer, wts, *, tm):
    x1, xs, gs, pos, meta = _dispatch(attn2d, x2d, wts["w_o"][layer], wts["ln_g"][layer, 0:1],
                                      wts["ln_b"][layer, 0:1], wts["w_router"], wts["b_router"],
                                      wts["tri"], tm=tm)
    ys = _experts(meta[:, :, 0].reshape(-1), xs, gs, wts["w_gate_t"][layer], wts["w_up_t"][layer],
                  wts["w_down_t"][layer])
    return _combine(ys, pos, x1, wts["ln_g"][layer, 1:2], wts["ln_b"][layer, 1:2], p2d,
                    wts["w_ple"][layer], wts["w_ple_gate"][layer], tm=tm)


def _token_tile(t):
    return 512 if t % 512 == 0 else t


def _ffn_block(attn2d, x2d, p2d, layer, wts, *, tm):
    if x2d.shape[0] % MOE_TILE == 0:
        return _routed_ffn_block(attn2d, x2d, p2d, layer, wts, tm=MOE_TILE)
    x1, gates = _post_attn(attn2d, x2d, wts["w_o"][layer], wts["ln_g"][layer, 0:1], wts["ln_b"][layer, 0:1],
                           wts["w_router"], wts["b_router"], tm=tm)
    return _moe(x1, gates, wts["w_gate_t"][layer], wts["w_up_t"][layer], wts["w_down_t"][layer],
                wts["ln_g"][layer, 1:2], wts["ln_b"][layer, 1:2], p2d, wts["w_ple"][layer],
                wts["w_ple_gate"][layer], tm=tm)


def _prompt_trunk(x, p, wts):
    b, s, _ = x.shape
    t = b * s
    tm = _token_tile(s)
    cs = _rope_tables(jnp.arange(s))
    x2d = x.reshape(t, D_MODEL)

    q, k0, v0, khm, vt, km = _proj_rope(x2d, wts["w_qkv"][0], cs, seq=s, tm=tm, attn_layout=True, with_kmean=True)
    kmean = km.reshape(b, s // MOBA_BLOCK, KV_HEADS, HEAD_DIM).transpose(0, 2, 1, 3)
    attn = _moba_prompt(q.reshape(b, s, D_MODEL), khm, vt, kmean)
    x2d = _ffn_block(attn.reshape(t, D_MODEL), x2d, p[0].reshape(t, PLE_DIM), 0, wts, tm=tm)

    q, k1, v1, khm, vt = _proj_rope(x2d, wts["w_qkv"][1], cs, seq=s, tm=tm, attn_layout=True, with_kmean=False)
    attn = _swa_prompt(q.reshape(b, s, D_MODEL), khm, vt, wts["sinks"])
    x2d = _ffn_block(attn.reshape(t, D_MODEL), x2d, p[1].reshape(t, PLE_DIM), 1, wts, tm=tm)
    return (x2d.reshape(b, s, D_MODEL), k0.reshape(b, s, KV_HEADS, HEAD_DIM), v0.reshape(b, s, KV_HEADS, HEAD_DIM),
            k1.reshape(b, s, KV_HEADS, HEAD_DIM), v1.reshape(b, s, KV_HEADS, HEAD_DIM))


def _sample_trunk(x, p, cache_k, cache_v, state_k, state_v, page_table, wts):
    db, tn, _ = x.shape
    t = db * tn
    tm = _token_tile(t)
    pos = PAST_LEN + jnp.arange(tn)
    cs = jnp.tile(_rope_tables(pos), (db, 1))
    x2d = x.reshape(t, D_MODEL)

    q, k0, v0 = _proj_rope(x2d, wts["w_qkv"][0], cs, seq=tn, tm=tm, attn_layout=False, with_kmean=False)
    o = _moba_sample(_block_diag_q(q.reshape(db, tn, D_MODEL)),
                     _pad_new(k0.reshape(db, tn, KV_DIM)), _pad_new(v0.reshape(db, tn, KV_DIM)),
                     _cache_pages_t(cache_k), _cache_pages_t(cache_v), page_table)
    attn = _block_diag_out(o, tn).astype(BF16)
    x2d = _ffn_block(attn.reshape(t, D_MODEL), x2d, p[0].reshape(t, PLE_DIM), 0, wts, tm=tm)

    q, k1, v1 = _proj_rope(x2d, wts["w_qkv"][1], cs, seq=tn, tm=tm, attn_layout=False, with_kmean=False)
    wb = state_k.shape[1]
    sink_rows = jnp.broadcast_to(
        jnp.broadcast_to(wts["sinks"].reshape(KV_HEADS, 1, GROUP), (KV_HEADS, tn, GROUP)).reshape(SAMPLE_ROWS, 1),
        (SAMPLE_ROWS, LANES))
    o = _swa_sample(_block_diag_q(q.reshape(db, tn, D_MODEL)),
                    state_k.reshape(db, wb, KV_DIM), state_v.reshape(db, wb, KV_DIM),
                    _pad_new(k1.reshape(db, tn, KV_DIM)), _pad_new(v1.reshape(db, tn, KV_DIM)), sink_rows)
    attn = _block_diag_out(o, tn).astype(BF16)
    x2d = _ffn_block(attn.reshape(t, D_MODEL), x2d, p[1].reshape(t, PLE_DIM), 1, wts, tm=tm)
    return (x2d.reshape(db, tn, D_MODEL), k0.reshape(db, tn, KV_HEADS, HEAD_DIM), v0.reshape(db, tn, KV_HEADS, HEAD_DIM),
            k1.reshape(db, tn, KV_HEADS, HEAD_DIM), v1.reshape(db, tn, KV_HEADS, HEAD_DIM))


def _split_hi_lo(w):
    hi = w.astype(BF16)
    lo = (w - hi.astype(F32)).astype(BF16)
    return jnp.concatenate([hi, lo], axis=0)


def _cache_pages_t(cache):
    pool, page = cache.shape[1], cache.shape[2]
    return jnp.transpose(cache.reshape(pool, page, KV_HEADS, HEAD_DIM), (0, 2, 3, 1)).reshape(pool, KV_DIM, page)


def kernel(x_prompt, x_sample, p_prompt, p_sample, cache_k_a, cache_v_a, state_swa_k, state_swa_v, page_table,
           w_qkv_a, w_o_a, w_kv_s, w_q_b, w_o_b, sinks_b, ln_g, ln_b, w_router, b_router,
           w_exp_gate, w_exp_up, w_exp_down, w_ple, w_ple_gate):
    wts = {
        "w_qkv": jnp.stack([w_qkv_a[0], jnp.concatenate([w_q_b[0], w_kv_s], axis=1)]).astype(BF16),
        "w_o": jnp.stack([w_o_a[0], w_o_b[0]]).astype(BF16),
        "sinks": sinks_b[0].astype(F32),
        "ln_g": ln_g, "ln_b": ln_b,
        "w_router": _split_hi_lo(w_router.astype(F32).T),
        "b_router": b_router.astype(F32).reshape(N_EXPERTS, 1),
        "w_gate_t": jnp.swapaxes(w_exp_gate, 2, 3).astype(BF16),
        "w_up_t": jnp.swapaxes(w_exp_up, 2, 3).astype(BF16),
        "w_down_t": jnp.swapaxes(w_exp_down, 2, 3).astype(BF16),
        "tri": jnp.triu(jnp.ones((MOE_TILE, MOE_TILE), BF16), k=1),
        "w_ple": w_ple.astype(BF16), "w_ple_gate": w_ple_gate.astype(BF16),
    }
    y_p, ka_p, va_p, ks_p, vs_p = _prompt_trunk(x_prompt, p_prompt, wts)
    y_s, ka_s, va_s, ks_s, vs_s = _sample_trunk(x_sample, p_sample, cache_k_a, cache_v_a,
                                                state_swa_k, state_swa_v, page_table, wts)
    wb_p = min(WINDOW, x_prompt.shape[1])
    tn = x_sample.shape[1]
    return (y_p, y_s, ka_p[None], va_p[None], ka_s[None], va_s[None],
            ks_p[:, -wb_p:], vs_p[:, -wb_p:],
            jnp.concatenate([state_swa_k, ks_s], axis=1)[:, tn:],
            jnp.concatenate([state_swa_v, vs_s], axis=1)[:, tn:])
```

```python
import functools

import jax
import jax.numpy as jnp
from jax import lax
from jax.experimental import pallas as pl
from jax.experimental.pallas import tpu as pltpu

D_MODEL = 1024
HEAD_DIM = 64
HALF_DIM = HEAD_DIM // 2
N_HEADS = D_MODEL // HEAD_DIM
KV_HEADS = 4
GROUP = N_HEADS // KV_HEADS
KV_DIM = KV_HEADS * HEAD_DIM
MOBA_BLOCK = 256
BLOCK_SHIFT = MOBA_BLOCK.bit_length() - 1
MOBA_TOPK = 3
WINDOW = 128
PAGE_SIZE = 128
PAST_LEN = 16384
ROPE_THETA = 10000.0
N_EXPERTS = 16
N_GROUPS = 4
EXPERTS_PER_GROUP = N_EXPERTS // N_GROUPS
D_EXPERT = D_MODEL // 2
PLE_DIM = 256
DEPTH = 2
DEEPNORM_ALPHA = (2.0 * DEPTH) ** 0.25
LN_EPS = 1e-5
ATTN_SCALE = HEAD_DIM ** -0.5
LOG2E = 1.4426950408889634
Q_SCALE = ATTN_SCALE * LOG2E

LANES = 128
Q_ROWS = GROUP * MOBA_BLOCK
SAMPLE_ROWS = KV_HEADS * GROUP * 4
NEW_PAD = 8
NEW_ROWS = 128
PAGES_PER_ITEM = 32
DENOM_ROWS = 16
VMEM_LIMIT = 56 * 1024 * 1024

F32 = jnp.float32
BF16 = jnp.bfloat16
NEG_INF = float("-inf")
NT_DIMS = (((1,), (1,)), ((), ()))


def _nt_dot(a, b):
    return lax.dot_general(a, b, NT_DIMS, preferred_element_type=F32)


def _params(semantics):
    return pltpu.CompilerParams(dimension_semantics=semantics, vmem_limit_bytes=VMEM_LIMIT)


def _layer_norm(z, g, b):
    mu = jnp.mean(z, axis=-1, keepdims=True)
    zc = z - mu
    var = jnp.mean(zc * zc, axis=-1, keepdims=True)
    return zc * lax.rsqrt(var + LN_EPS) * g + b


def _rope_tables(pos):
    inv_freq = ROPE_THETA ** (-jnp.arange(HALF_DIM, dtype=F32) / HALF_DIM)
    ang = pos.astype(F32)[:, None] * inv_freq[None, :]
    cos = jnp.cos(ang)
    sin = jnp.sin(ang)
    cos_t = jnp.tile(cos, (1, 4))
    sin_t = jnp.tile(jnp.concatenate([-sin, sin], axis=1), (1, 2))
    return jnp.concatenate([cos_t, sin_t], axis=1)


def _proj_rope_kernel(x_ref, w_ref, cs_ref, q_ref, k_ref, v_ref, *rest, attn_layout, with_kmean):
    x = x_ref[...].astype(BF16)
    qkv = jnp.dot(x, w_ref[...], preferred_element_type=F32)
    tm = x.shape[0]
    cos = cs_ref[:, :LANES]
    sin = cs_ref[:, LANES:]
    lane = lax.broadcasted_iota(jnp.int32, (tm, LANES), 1)
    first_half = (lane & HALF_DIM) == 0

    def rope(blk):
        partner = jnp.where(first_half, pltpu.roll(blk, LANES - HALF_DIM, 1), pltpu.roll(blk, HALF_DIM, 1))
        return blk * cos + partner * sin

    for j in range(D_MODEL // LANES):
        q_ref[:, j * LANES:(j + 1) * LANES] = (rope(qkv[:, j * LANES:(j + 1) * LANES]) * Q_SCALE).astype(BF16)
    k = jnp.concatenate([rope(qkv[:, D_MODEL + j * LANES:D_MODEL + (j + 1) * LANES])
                         for j in range(KV_DIM // LANES)], axis=1)
    v = qkv[:, D_MODEL + KV_DIM:]
    if attn_layout:
        khm_ref, vt_ref = rest[0], rest[1]
        v_t = v.T
        k_ref[0] = k.T
        v_ref[0] = v_t
        for h in range(KV_HEADS):
            khm_ref[0, h] = k[:, h * HEAD_DIM:(h + 1) * HEAD_DIM].astype(BF16)
        vt_ref[0] = v_t.astype(BF16)
    else:
        k_ref[...] = k
        v_ref[...] = v
    if with_kmean:
        km_ref = rest[2]
        nb = tm // MOBA_BLOCK
        km_ref[0] = jnp.sum(k.reshape(nb, MOBA_BLOCK, KV_DIM), axis=1) * (1.0 / MOBA_BLOCK)


def _proj_rope(x2d, w_bf, cs, *, seq, tm, attn_layout, with_kmean):
    t = x2d.shape[0]
    nt = t // tm
    tab_tiles = cs.shape[0] // tm
    out_shape = [jax.ShapeDtypeStruct((t, D_MODEL), BF16)]
    out_specs = [pl.BlockSpec((tm, D_MODEL), lambda i: (i, 0))]
    if attn_layout:
        nseq = t // seq
        tiles_per_seq = seq // tm
        seq_tile = pl.BlockSpec((1, KV_DIM, tm), lambda i: (i // tiles_per_seq, 0, i % tiles_per_seq))
        out_shape += [jax.ShapeDtypeStruct((nseq, KV_DIM, seq), F32),
                      jax.ShapeDtypeStruct((nseq, KV_DIM, seq), F32),
                      jax.ShapeDtypeStruct((nseq, KV_HEADS, seq, HEAD_DIM), BF16),
                      jax.ShapeDtypeStruct((nseq, KV_DIM, seq), BF16)]
        out_specs += [seq_tile, seq_tile,
                      pl.BlockSpec((1, KV_HEADS, tm, HEAD_DIM),
                                   lambda i: (i // tiles_per_seq, 0, i % tiles_per_seq, 0)),
                      seq_tile]
    else:
        out_shape += [jax.ShapeDtypeStruct((t, KV_DIM), F32), jax.ShapeDtypeStruct((t, KV_DIM), F32)]
        out_specs += [pl.BlockSpec((tm, KV_DIM), lambda i: (i, 0)), pl.BlockSpec((tm, KV_DIM), lambda i: (i, 0))]
    if with_kmean:
        out_shape.append(jax.ShapeDtypeStruct((nt, tm // MOBA_BLOCK, KV_DIM), F32))
        out_specs.append(pl.BlockSpec((1, tm // MOBA_BLOCK, KV_DIM), lambda i: (i, 0, 0)))
    return pl.pallas_call(
        functools.partial(_proj_rope_kernel, attn_layout=attn_layout, with_kmean=with_kmean),
        grid=(nt,),
        in_specs=[pl.BlockSpec((tm, D_MODEL), lambda i: (i, 0)),
                  pl.BlockSpec(w_bf.shape, lambda i: (0, 0)),
                  pl.BlockSpec((tm, 2 * LANES), lambda i: (i % tab_tiles, 0))],
        out_specs=out_specs,
        out_shape=out_shape,
        compiler_params=_params(("parallel",)),
        name="proj_rope",
    )(x2d, w_bf, cs)


def _stack_heads(qb):
    return jnp.concatenate([qb[:, g * HEAD_DIM:(g + 1) * HEAD_DIM] for g in range(GROUP)], axis=0)


def _unstack_out(out_t):
    o2 = jnp.concatenate([out_t[:, g * MOBA_BLOCK:(g + 1) * MOBA_BLOCK] for g in range(GROUP)], axis=0)
    return o2.T.astype(BF16)


def _lane_chunks():
    return [slice(c * LANES, (c + 1) * LANES) for c in range(Q_ROWS // LANES)]


def _moba_prompt_kernel(q_ref, k_ref, vt_ref, km_ref, o_ref,
                        sel_ref, m_ref, a_ref, p_ref, acc_ref, s_ref, bm_ref):
    n = pl.program_id(2)
    nblk = km_ref.shape[2]
    qrows = _stack_heads(q_ref[0])

    gate = _nt_dot(km_ref[0, 0].astype(BF16), qrows)
    jio = lax.broadcasted_iota(jnp.int32, (nblk, Q_ROWS), 0)
    past = jio < n
    g = jnp.where(past, gate, NEG_INF)
    sel = jnp.zeros((nblk, Q_ROWS), jnp.bool_)
    for _ in range(MOBA_TOPK):
        mx = jnp.max(g, axis=0, keepdims=True)
        idx = jnp.min(jnp.where(g == mx, jio, nblk), axis=0, keepdims=True)
        hit = jio == idx
        sel = sel | hit
        g = jnp.where(hit, NEG_INF, g)
    sel_ref[...] = (sel & past).astype(F32)

    own = pl.multiple_of(n * MOBA_BLOCK, MOBA_BLOCK)
    k_own = k_ref[0, 0, pl.ds(own, MOBA_BLOCK), :]
    kio = lax.broadcasted_iota(jnp.int32, (MOBA_BLOCK, LANES), 0)
    lio = lax.broadcasted_iota(jnp.int32, (MOBA_BLOCK, LANES), 1)
    for ci, cs in enumerate(_lane_chunks()):
        s = _nt_dot(k_own, qrows[cs, :])
        qio = (lio + ci * LANES) & (MOBA_BLOCK - 1)
        s = jnp.where(kio <= qio, s, NEG_INF)
        m0 = jnp.max(s, axis=0, keepdims=True)
        p = jnp.exp2(s - m0)
        m_ref[:, cs] = m0
        p_ref[:, cs] = p.astype(BF16)
    acc_ref[...] = jnp.zeros_like(acc_ref)
    ones_rows = jnp.ones((DENOM_ROWS, MOBA_BLOCK), BF16)

    def pv(blk):
        off = pl.multiple_of(blk * MOBA_BLOCK, MOBA_BLOCK)
        v_ext = jnp.concatenate([vt_ref[0, :, pl.ds(off, MOBA_BLOCK)], ones_rows], axis=0)
        return jnp.dot(v_ext, p_ref[...], preferred_element_type=F32)

    def scores(j, slot):
        off = pl.multiple_of(j * MOBA_BLOCK, MOBA_BLOCK)
        k_j = k_ref[0, 0, pl.ds(off, MOBA_BLOCK), :]
        for cs in _lane_chunks():
            sj = _nt_dot(k_j, qrows[cs, :])
            s_ref[slot, :, cs] = sj
            bm_ref[slot, :, cs] = jnp.max(sj, axis=0, keepdims=True)

    def softmax(j, slot):
        pv_prev = pv(jnp.where(j == 0, n, j - 1))
        sel_j = sel_ref[pl.ds(j, 1), :]
        for cs in _lane_chunks():
            picked = sel_j[:, cs] > 0.0
            m_old = m_ref[:, cs]
            m_new = jnp.where(picked, jnp.maximum(m_old, bm_ref[slot, :, cs]), m_old)
            a = jnp.exp2(m_old - m_new)
            pj = jnp.exp2(s_ref[slot, :, cs] - jnp.where(picked, m_new, jnp.inf))
            m_ref[:, cs] = m_new
            a_ref[:, cs] = a
            p_ref[:, cs] = pj.astype(BF16)
        acc_ref[...] = a_ref[...] * (acc_ref[...] + pv_prev)

    @pl.when(n > 0)
    def _():
        scores(0, 0)

    def block_pair(i, carry):
        j = 2 * i
        scores(j + 1, 1)
        softmax(j, 0)
        scores(jnp.minimum(j + 2, n - 1), 0)
        softmax(j + 1, 1)
        return carry

    lax.fori_loop(0, n >> 1, block_pair, 0)

    @pl.when((n & 1) == 1)
    def _():
        softmax(n - 1, 0)

    last = jnp.where(n == 0, 0, n - 1)
    total = acc_ref[...] + pv(last)
    o_ref[0] = _unstack_out(total[:HEAD_DIM] / total[HEAD_DIM:HEAD_DIM + 1])


def _moba_prompt(q, khm, vt, kmean):
    b, s, _ = q.shape
    nblk = s // MOBA_BLOCK
    return pl.pallas_call(
        _moba_prompt_kernel,
        grid=(b, KV_HEADS, nblk),
        in_specs=[pl.BlockSpec((1, MOBA_BLOCK, KV_DIM), lambda bi, h, n: (bi, n, h)),
                  pl.BlockSpec((1, 1, s, HEAD_DIM), lambda bi, h, n: (bi, h, 0, 0)),
                  pl.BlockSpec((1, HEAD_DIM, s), lambda bi, h, n: (bi, h, 0)),
                  pl.BlockSpec((1, 1, nblk, HEAD_DIM), lambda bi, h, n: (bi, h, 0, 0))],
        out_specs=pl.BlockSpec((1, MOBA_BLOCK, KV_DIM), lambda bi, h, n: (bi, n, h)),
        out_shape=jax.ShapeDtypeStruct((b, s, D_MODEL), BF16),
        scratch_shapes=[pltpu.VMEM((nblk, Q_ROWS), F32),
                        pltpu.VMEM((1, Q_ROWS), F32),
                        pltpu.VMEM((1, Q_ROWS), F32),
                        pltpu.VMEM((MOBA_BLOCK, Q_ROWS), BF16),
                        pltpu.VMEM((HEAD_DIM + DENOM_ROWS, Q_ROWS), F32),
                        pltpu.VMEM((2, MOBA_BLOCK, Q_ROWS), F32),
                        pltpu.VMEM((2, 1, Q_ROWS), F32)],
        compiler_params=_params(("parallel", "parallel", "arbitrary")),
        name="moba_prompt",
    )(q, khm, vt, kmean)


SWA_KEYS = MOBA_BLOCK + WINDOW


def _swa_prompt_kernel(sink_ref, q_ref, k_ref, vt_ref, o_ref, d_ref, p_ref, bias_ref, s_ref, m_ref):
    h = pl.program_id(1)
    n = pl.program_id(2)
    qrows = _stack_heads(q_ref[0])
    start = pl.multiple_of(jnp.maximum(n * MOBA_BLOCK - WINDOW, 0), WINDOW)
    k_win = k_ref[0, 0, pl.ds(start, SWA_KEYS), :]

    @pl.when(n <= 1)
    def _():
        kpos = start + lax.broadcasted_iota(jnp.int32, (SWA_KEYS, LANES), 0)
        lio = lax.broadcasted_iota(jnp.int32, (SWA_KEYS, LANES), 1)
        for ci, cs in enumerate(_lane_chunks()):
            diff = n * MOBA_BLOCK + ((lio + ci * LANES) & (MOBA_BLOCK - 1)) - kpos
            bias_ref[:, cs] = jnp.where((diff >= 0) & (diff < WINDOW), 0.0, NEG_INF)

    for ci, cs in enumerate(_lane_chunks()):
        s = _nt_dot(k_win, qrows[cs, :]) + bias_ref[:, cs]
        sink = sink_ref[h * GROUP + (ci * LANES) // MOBA_BLOCK] * LOG2E
        m = jnp.maximum(jnp.max(s, axis=0, keepdims=True), sink)
        s_ref[:, cs] = s
        m_ref[:, cs] = m
        d_ref[:, cs] = jnp.exp2(sink - m)
    for cs in _lane_chunks():
        p_ref[:, cs] = jnp.exp2(s_ref[:, cs] - m_ref[:, cs]).astype(BF16)
    v_ext = jnp.concatenate([vt_ref[0, :, pl.ds(start, SWA_KEYS)], jnp.ones((DENOM_ROWS, SWA_KEYS), BF16)], axis=0)
    total = jnp.dot(v_ext, p_ref[...], preferred_element_type=F32)
    o_ref[0] = _unstack_out(total[:HEAD_DIM] / (total[HEAD_DIM:HEAD_DIM + 1] + d_ref[...]))


def _swa_prompt(q, khm, vt, sinks):
    b, s, _ = q.shape
    nblk = s // MOBA_BLOCK
    grid_spec = pltpu.PrefetchScalarGridSpec(
        num_scalar_prefetch=1,
        grid=(b, KV_HEADS, nblk),
        in_specs=[pl.BlockSpec((1, MOBA_BLOCK, KV_DIM), lambda bi, h, n, sk: (bi, n, h)),
                  pl.BlockSpec((1, 1, s, HEAD_DIM), lambda bi, h, n, sk: (bi, h, 0, 0)),
                  pl.BlockSpec((1, HEAD_DIM, s), lambda bi, h, n, sk: (bi, h, 0))],
        out_specs=pl.BlockSpec((1, MOBA_BLOCK, KV_DIM), lambda bi, h, n, sk: (bi, n, h)),
        scratch_shapes=[pltpu.VMEM((1, Q_ROWS), F32),
                        pltpu.VMEM((SWA_KEYS, Q_ROWS), BF16),
                        pltpu.VMEM((SWA_KEYS, Q_ROWS), F32),
                        pltpu.VMEM((SWA_KEYS, Q_ROWS), F32),
                        pltpu.VMEM((1, Q_ROWS), F32)],
    )
    return pl.pallas_call(
        _swa_prompt_kernel,
        grid_spec=grid_spec,
        out_shape=jax.ShapeDtypeStruct((b, s, D_MODEL), BF16),
        compiler_params=_params(("parallel", "parallel", "arbitrary")),
        name="swa_prompt",
    )(sinks, q, khm, vt)


def _block_diag_q(q):
    db, t, _ = q.shape
    q5 = q.reshape(db, t, KV_HEADS, GROUP, HEAD_DIM).transpose(0, 2, 1, 3, 4)
    q5 = q5.reshape(db, KV_HEADS, t * GROUP, HEAD_DIM)
    eye = jnp.eye(KV_HEADS, dtype=q.dtype)
    return jnp.einsum("ikrd,kc->ikrcd", q5, eye).reshape(db, KV_HEADS * t * GROUP, KV_DIM)


def _block_diag_out(o, t):
    db = o.shape[0]
    o5 = o.reshape(db, KV_HEADS, t * GROUP, KV_HEADS, HEAD_DIM)
    od = jnp.stack([o5[:, h, :, h, :] for h in range(KV_HEADS)], axis=1)
    return od.reshape(db, KV_HEADS, t, GROUP, HEAD_DIM).transpose(0, 2, 1, 3, 4).reshape(db, t, D_MODEL)


def _pad_new(x):
    return jnp.pad(x, ((0, 0), (0, NEW_PAD - x.shape[1]), (0, 0)))


def _row_token(shape):
    return (lax.broadcasted_iota(jnp.int32, shape, 0) >> 2) & 3


def _new_scores(qbd, knew):
    pad = jnp.zeros((NEW_ROWS - NEW_PAD, KV_DIM), BF16)
    s_new = _nt_dot(qbd, jnp.concatenate([knew.astype(BF16), pad], axis=0))
    tcol = lax.broadcasted_iota(jnp.int32, s_new.shape, 1)
    return s_new, tcol <= _row_token(s_new.shape)


def _new_token_part(e_new, v_new):
    out = e_new[:, 0:1] * v_new[0:1, :]
    for t in range(1, 4):
        out = out + e_new[:, t:t + 1] * v_new[t:t + 1, :]
    return out


def _moba_sample_kernel(pt_ref, qbd_ref, knew_ref, vnew_ref, ck_ref, cv_ref, o_ref,
                        buf, sem, s_all, p_all, gate_s, acc_s):
    i = pl.program_id(0)
    ns = pl.num_programs(0)
    n_pages = pt_ref.shape[1]
    k_items = n_pages // PAGES_PER_ITEM
    n_items = 2 * k_items
    item_cols = PAGES_PER_ITEM * PAGE_SIZE
    blocks_per_item = item_cols // MOBA_BLOCK
    n_blocks = k_items * blocks_per_item
    nb_pad = gate_s.shape[1]
    qbd = qbd_ref[0]

    def page_copy(src, page, slot, pg):
        return pltpu.make_async_copy(src.at[page], buf.at[slot, :, pl.ds(pg * PAGE_SIZE, PAGE_SIZE)],
                                     sem.at[slot])

    def start_item(sample, it, slot):
        src = ck_ref if it < k_items else cv_ref
        c = it % k_items
        for pg in range(PAGES_PER_ITEM):
            page_copy(src, pt_ref[sample, c * PAGES_PER_ITEM + pg], slot, pg).start()

    def wait_item(slot):
        for pg in range(PAGES_PER_ITEM):
            page_copy(ck_ref, 0, slot, pg).wait()

    @pl.when(i == 0)
    def _():
        start_item(0, 0, 0)

    gate_s[...] = jnp.zeros_like(gate_s)
    glane = lax.broadcasted_iota(jnp.int32, (SAMPLE_ROWS, nb_pad), 1)
    own_out = None
    l_tot = None
    for it in range(n_items):
        slot = it % 2
        if it + 1 < n_items:
            start_item(i, it + 1, 1 - slot)
        else:
            @pl.when(i + 1 < ns)
            def _():
                start_item(i + 1, 0, 1 - slot)
        wait_item(slot)

        if it < k_items:
            s = jnp.dot(qbd, buf[slot].astype(BF16), preferred_element_type=F32)
            s_all[:, it * item_cols:(it + 1) * item_cols] = s
            gate = gate_s[...]
            for bi in range(blocks_per_item):
                gate = jnp.where(glane == it * blocks_per_item + bi,
                                 jnp.sum(s[:, bi * MOBA_BLOCK:(bi + 1) * MOBA_BLOCK], axis=1, keepdims=True), gate)
            gate_s[...] = gate

        if it == k_items - 1:
            g = jnp.where(glane < n_blocks, gate_s[...], NEG_INF)
            sel = jnp.zeros((SAMPLE_ROWS, nb_pad), jnp.bool_)
            for _ in range(min(MOBA_TOPK, n_blocks)):
                mx = jnp.max(g, axis=1, keepdims=True)
                idx = jnp.min(jnp.where(g == mx, glane, nb_pad), axis=1, keepdims=True)
                hit = glane == idx
                sel = sel | hit
                g = jnp.where(hit, NEG_INF, g)
            sel_bf = sel.astype(BF16)

            s_new, new_ok = _new_scores(qbd, knew_ref[0])
            m = jnp.max(jnp.where(new_ok, s_new, NEG_INF), axis=1, keepdims=True)

            def expand(c):
                blk = lax.broadcasted_iota(jnp.int32, (nb_pad, item_cols), 0)
                key_blk = lax.broadcasted_iota(jnp.int32, (nb_pad, item_cols), 1) >> BLOCK_SHIFT
                ex = (blk == key_blk + c * blocks_per_item).astype(BF16)
                return jnp.dot(sel_bf, ex, preferred_element_type=F32) > 0.5

            for c in range(k_items):
                sc = jnp.where(expand(c), s_all[:, c * item_cols:(c + 1) * item_cols], NEG_INF)
                m = jnp.maximum(m, jnp.max(sc, axis=1, keepdims=True))
            e_new = jnp.where(new_ok, jnp.exp2(s_new - m), 0.0)
            l_tot = jnp.sum(e_new, axis=1, keepdims=True)
            for c in range(k_items):
                sc = jnp.where(expand(c), s_all[:, c * item_cols:(c + 1) * item_cols], NEG_INF)
                pc = jnp.exp2(sc - m)
                l_tot = l_tot + jnp.sum(pc, axis=1, keepdims=True)
                p_all[:, c * item_cols:(c + 1) * item_cols] = pc.astype(BF16)
            own_out = _new_token_part(e_new, vnew_ref[0])
            acc_s[...] = jnp.zeros_like(acc_s)

        if it >= k_items:
            c = it - k_items
            acc_s[...] += _nt_dot(p_all[:, c * item_cols:(c + 1) * item_cols], buf[slot].astype(BF16))

    o_ref[0] = (acc_s[...] + own_out) / l_tot


def _moba_sample(qbd, k_new, v_new, cache_kt, cache_vt, page_table):
    db = qbd.shape[0]
    n_pages = page_table.shape[1]
    past = n_pages * PAGE_SIZE
    item_cols = PAGES_PER_ITEM * PAGE_SIZE
    nb_pad = -(-(past // MOBA_BLOCK) // LANES) * LANES
    grid_spec = pltpu.PrefetchScalarGridSpec(
        num_scalar_prefetch=1,
        grid=(db,),
        in_specs=[pl.BlockSpec((1, SAMPLE_ROWS, KV_DIM), lambda i, pt: (i, 0, 0)),
                  pl.BlockSpec((1, NEW_PAD, KV_DIM), lambda i, pt: (i, 0, 0)),
                  pl.BlockSpec((1, NEW_PAD, KV_DIM), lambda i, pt: (i, 0, 0)),
                  pl.BlockSpec(memory_space=pl.ANY),
                  pl.BlockSpec(memory_space=pl.ANY)],
        out_specs=pl.BlockSpec((1, SAMPLE_ROWS, KV_DIM), lambda i, pt: (i, 0, 0)),
        scratch_shapes=[pltpu.VMEM((2, KV_DIM, item_cols), F32),
                        pltpu.SemaphoreType.DMA((2,)),
                        pltpu.VMEM((SAMPLE_ROWS, past), F32),
                        pltpu.VMEM((SAMPLE_ROWS, past), BF16),
                        pltpu.VMEM((SAMPLE_ROWS, nb_pad), F32),
                        pltpu.VMEM((SAMPLE_ROWS, KV_DIM), F32)],
    )
    return pl.pallas_call(
        _moba_sample_kernel,
        grid_spec=grid_spec,
        out_shape=jax.ShapeDtypeStruct((db, SAMPLE_ROWS, KV_DIM), F32),
        compiler_params=_params(("arbitrary",)),
        name="moba_sample",
    )(page_table, qbd, k_new, v_new, cache_kt, cache_vt)


def _swa_sample_kernel(qbd_ref, kbuf_ref, vbuf_ref, knew_ref, vnew_ref, sink_ref, o_ref):
    wb = kbuf_ref.shape[2]
    bcol = lax.broadcasted_iota(jnp.int32, (SAMPLE_ROWS, wb), 1)
    bdiff = wb + _row_token((SAMPLE_ROWS, wb)) - bcol
    buf_ok = (bdiff >= 0) & (bdiff < WINDOW)
    sink = sink_ref[:, 0:1] * LOG2E
    for b in range(qbd_ref.shape[0]):
        qbd = qbd_ref[b]
        s_buf = jnp.dot(qbd, kbuf_ref[b].astype(BF16), preferred_element_type=F32)
        s_new, new_ok = _new_scores(qbd, knew_ref[b])
        m = jnp.maximum(jnp.max(jnp.where(buf_ok, s_buf, NEG_INF), axis=1, keepdims=True),
                        jnp.max(jnp.where(new_ok, s_new, NEG_INF), axis=1, keepdims=True))
        m = jnp.maximum(m, sink)
        e_buf = jnp.where(buf_ok, jnp.exp2(s_buf - m), 0.0)
        e_new = jnp.where(new_ok, jnp.exp2(s_new - m), 0.0)
        denom = (jnp.sum(e_buf, axis=1, keepdims=True) + jnp.sum(e_new, axis=1, keepdims=True)
                 + jnp.exp2(sink - m))
        out = _nt_dot(e_buf.astype(BF16), vbuf_ref[b].astype(BF16))
        o_ref[b] = (out + _new_token_part(e_new, vnew_ref[b])) / denom


def _swa_sample(qbd, k_buf_t, v_buf_t, k_new, v_new, sink_rows):
    db, _, wb = k_buf_t.shape
    sb = next(c for c in (8, 4, 2, 1) if db % c == 0)
    return pl.pallas_call(
        _swa_sample_kernel,
        grid=(db // sb,),
        in_specs=[pl.BlockSpec((sb, SAMPLE_ROWS, KV_DIM), lambda i: (i, 0, 0)),
                  pl.BlockSpec((sb, KV_DIM, wb), lambda i: (i, 0, 0)),
                  pl.BlockSpec((sb, KV_DIM, wb), lambda i: (i, 0, 0)),
                  pl.BlockSpec((sb, NEW_PAD, KV_DIM), lambda i: (i, 0, 0)),
                  pl.BlockSpec((sb, NEW_PAD, KV_DIM), lambda i: (i, 0, 0)),
                  pl.BlockSpec((SAMPLE_ROWS, LANES), lambda i: (0, 0))],
        out_specs=pl.BlockSpec((sb, SAMPLE_ROWS, KV_DIM), lambda i: (i, 0, 0)),
        out_shape=jax.ShapeDtypeStruct((db, SAMPLE_ROWS, KV_DIM), F32),
        compiler_params=_params(("parallel",)),
        name="swa_sample",
    )(qbd, k_buf_t, v_buf_t, k_new, v_new, sink_rows)


def _first_of(vals, m):
    idx = jnp.full(m.shape, EXPERTS_PER_GROUP - 1, jnp.int32)
    for k in range(EXPERTS_PER_GROUP - 2, -1, -1):
        idx = jnp.where(vals[k] == m, k, idx)
    return idx


def _route_picks(aff, bias):
    biased = aff + bias
    best = e1 = e2 = None
    for grp in range(N_GROUPS):
        v = [biased[grp * EXPERTS_PER_GROUP + k:grp * EXPERTS_PER_GROUP + k + 1, :] for k in range(EXPERTS_PER_GROUP)]
        m1 = jnp.maximum(jnp.maximum(v[0], v[1]), jnp.maximum(v[2], v[3]))
        i1 = _first_of(v, m1)
        v2 = [jnp.where(i1 == k, NEG_INF, v[k]) for k in range(EXPERTS_PER_GROUP)]
        m2 = jnp.maximum(jnp.maximum(v2[0], v2[1]), jnp.maximum(v2[2], v2[3]))
        i2 = _first_of(v2, m2)
        gs = m1 + m2
        if grp == 0:
            best, e1, e2 = gs, i1, i2
        else:
            better = gs > best
            best = jnp.where(better, gs, best)
            e1 = jnp.where(better, grp * EXPERTS_PER_GROUP + i1, e1)
            e2 = jnp.where(better, grp * EXPERTS_PER_GROUP + i2, e2)
    eio = lax.broadcasted_iota(jnp.int32, aff.shape, 0)
    sel1 = eio == e1
    sel2 = eio == e2
    a1 = jnp.sum(jnp.where(sel1, aff, 0.0), axis=0, keepdims=True)
    a2 = jnp.sum(jnp.where(sel2, aff, 0.0), axis=0, keepdims=True)
    den = a1 + a2
    return e1, e2, a1 / den, a2 / den


def _route_t(aff, bias):
    e1, e2, g1, g2 = _route_picks(aff, bias)
    eio = lax.broadcasted_iota(jnp.int32, aff.shape, 0)
    return jnp.where(eio == e1, g1, 0.0) + jnp.where(eio == e2, g2, 0.0)


def _router_affinity(x1, wr_ref):
    x_hi = x1.astype(BF16)
    x_lo = (x1 - x_hi.astype(F32)).astype(BF16)
    w2 = wr_ref[...]
    t_hi = _nt_dot(w2, x_hi)
    t_lo = _nt_dot(w2[:N_EXPERTS], x_lo)
    return jax.nn.sigmoid(t_hi[:N_EXPERTS] + t_hi[N_EXPERTS:] + t_lo)


def _post_attn_kernel(a_ref, x_ref, wo_ref, g_ref, b_ref, wr_ref, br_ref, x1_ref, gates_ref):
    o = jnp.dot(a_ref[...], wo_ref[...], preferred_element_type=F32)
    x1 = _layer_norm(DEEPNORM_ALPHA * x_ref[...] + o, g_ref[...], b_ref[...])
    x1_ref[...] = x1
    gates_t = _route_t(_router_affinity(x1, wr_ref), br_ref[...])
    tm = x1.shape[0]
    gates_ref[...] = jnp.concatenate([gates_t, jnp.zeros((LANES - N_EXPERTS, tm), F32)], axis=0).T


def _post_attn(attn, x2d, wo_bf, ln_g, ln_b, wr_split, br_col, *, tm):
    t = x2d.shape[0]
    return pl.pallas_call(
        _post_attn_kernel,
        grid=(t // tm,),
        in_specs=[pl.BlockSpec((tm, D_MODEL), lambda i: (i, 0)),
                  pl.BlockSpec((tm, D_MODEL), lambda i: (i, 0)),
                  pl.BlockSpec((D_MODEL, D_MODEL), lambda i: (0, 0)),
                  pl.BlockSpec((1, D_MODEL), lambda i: (0, 0)),
                  pl.BlockSpec((1, D_MODEL), lambda i: (0, 0)),
                  pl.BlockSpec((2 * N_EXPERTS, D_MODEL), lambda i: (0, 0)),
                  pl.BlockSpec((N_EXPERTS, 1), lambda i: (0, 0))],
        out_specs=[pl.BlockSpec((tm, D_MODEL), lambda i: (i, 0)),
                   pl.BlockSpec((tm, LANES), lambda i: (i, 0))],
        out_shape=[jax.ShapeDtypeStruct((t, D_MODEL), F32),
                   jax.ShapeDtypeStruct((t, LANES), F32)],
        compiler_params=_params(("parallel",)),
        name="post_attn",
    )(attn, x2d, wo_bf, ln_g, ln_b, wr_split, br_col)


def _moe_kernel(x1_ref, gates_ref, wg_ref, wu_ref, wd_ref, g_ref, b_ref, p_ref, wple_ref, wpg_ref,
                out_ref, xb_ref, acc_ref):
    e = pl.program_id(1)

    @pl.when(e == 0)
    def _():
        xb_ref[...] = x1_ref[...].astype(BF16)
        acc_ref[...] = jnp.zeros_like(acc_ref)

    xb = xb_ref[...]
    hg = _nt_dot(xb, wg_ref[0])
    hu = _nt_dot(xb, wu_ref[0])
    h = hg * jax.nn.sigmoid(hg) * hu
    y = _nt_dot(h.astype(BF16), wd_ref[0])
    gates = gates_ref[...]
    lane = lax.broadcasted_iota(jnp.int32, gates.shape, 1)
    gate_e = jnp.sum(jnp.where(lane == e, gates, 0.0), axis=1, keepdims=True)
    acc_ref[...] += gate_e * y

    @pl.when(e == pl.num_programs(1) - 1)
    def _():
        x2 = _layer_norm(DEEPNORM_ALPHA * x1_ref[...] + acc_ref[...], g_ref[...], b_ref[...])
        ple = jnp.dot(p_ref[...].astype(BF16), wple_ref[...], preferred_element_type=F32)
        gt = jax.nn.sigmoid(jnp.dot(x2.astype(BF16), wpg_ref[...], preferred_element_type=F32))
        out_ref[...] = x2 + ple * gt


def _moe(x1, gates, wg_bf, wu_bf, wd_bf, ln_g, ln_b, p2d, wple_bf, wpg_bf, *, tm):
    t = x1.shape[0]
    return pl.pallas_call(
        _moe_kernel,
        grid=(t // tm, N_EXPERTS),
        in_specs=[pl.BlockSpec((tm, D_MODEL), lambda i, e: (i, 0)),
                  pl.BlockSpec((tm, LANES), lambda i, e: (i, 0)),
                  pl.BlockSpec((1, D_EXPERT, D_MODEL), lambda i, e: (e, 0, 0)),
                  pl.BlockSpec((1, D_EXPERT, D_MODEL), lambda i, e: (e, 0, 0)),
                  pl.BlockSpec((1, D_MODEL, D_EXPERT), lambda i, e: (e, 0, 0)),
                  pl.BlockSpec((1, D_MODEL), lambda i, e: (0, 0)),
                  pl.BlockSpec((1, D_MODEL), lambda i, e: (0, 0)),
                  pl.BlockSpec((tm, PLE_DIM), lambda i, e: (i, 0)),
                  pl.BlockSpec((PLE_DIM, D_MODEL), lambda i, e: (0, 0)),
                  pl.BlockSpec((D_MODEL, D_MODEL), lambda i, e: (0, 0))],
        out_specs=pl.BlockSpec((tm, D_MODEL), lambda i, e: (i, 0)),
        out_shape=jax.ShapeDtypeStruct((t, D_MODEL), F32),
        scratch_shapes=[pltpu.VMEM((tm, D_MODEL), BF16),
                        pltpu.VMEM((tm, D_MODEL), F32)],
        compiler_params=_params(("parallel", "arbitrary")),
        name="moe",
    )(x1, gates, wg_bf, wu_bf, wd_bf, ln_g, ln_b, p2d, wple_bf, wpg_bf)


MOE_TILE = 1024
CHUNK = LANES
META_ROWS = 8


def _sorted_rows(tm):
    return tm + N_GROUPS * CHUNK


def _one_hot_rows(pos_row, n_rows):
    tm = pos_row.shape[1]
    jio = lax.broadcasted_iota(jnp.int32, (n_rows, tm), 0)
    return (jio == pos_row.astype(jnp.int32)).astype(BF16)


def _dispatch_kernel(a_ref, x_ref, wo_ref, g_ref, b_ref, wr_ref, br_ref, tri_ref,
                     x1_ref, xs_ref, gs_ref, pos_ref, meta_ref):
    o = jnp.dot(a_ref[...], wo_ref[...], preferred_element_type=F32)
    x1 = _layer_norm(DEEPNORM_ALPHA * x_ref[...] + o, g_ref[...], b_ref[...])
    x1_ref[...] = x1
    tm = x1.shape[0]
    n_rows = xs_ref.shape[2]
    e1, e2, g1, g2 = _route_picks(_router_affinity(x1, wr_ref), br_ref[...])
    gid = e1 >> 2
    k1 = e1 & (EXPERTS_PER_GROUP - 1)
    k2 = e2 & (EXPERTS_PER_GROUP - 1)

    gio = lax.broadcasted_iota(jnp.int32, (META_ROWS, tm), 0)
    oh = gio == gid
    ohf = oh.astype(F32)
    rank = jnp.dot(oh.astype(BF16), tri_ref[...], preferred_element_type=F32)
    count = jnp.sum(ohf, axis=1, keepdims=True).astype(jnp.int32)
    nch = (count + (CHUNK - 1)) >> (CHUNK.bit_length() - 1)
    starts = [jnp.zeros((1, 1), jnp.int32)]
    for grp in range(1, N_GROUPS):
        starts.append(starts[-1] + nch[grp - 1:grp, :])
    pos = jnp.zeros((1, tm), F32)
    for grp in range(N_GROUPS):
        pos = pos + ohf[grp:grp + 1, :] * (rank[grp:grp + 1, :] + (starts[grp] * CHUNK).astype(F32))
    pos_ref[0] = jnp.broadcast_to(pos, (META_ROWS, tm))
    meta_ref[0] = jnp.concatenate(
        [jnp.broadcast_to(starts[grp], (1, LANES)) for grp in range(N_GROUPS)]
        + [jnp.broadcast_to(nch[grp:grp + 1, :], (1, LANES)) for grp in range(N_GROUPS)], axis=0)

    p_mat = _one_hot_rows(pos, n_rows)
    xs_ref[0] = _nt_dot(x1.T.astype(BF16), p_mat).astype(BF16)

    kio = lax.broadcasted_iota(jnp.int32, (META_ROWS, tm), 0)
    gk = jnp.where(kio == k1, g1, 0.0) + jnp.where(kio == k2, g2, 0.0)
    p1 = gk.astype(BF16).astype(F32)
    r1 = gk - p1
    p2 = r1.astype(BF16).astype(F32)
    p3 = r1 - p2
    pieces = _nt_dot(jnp.concatenate([p1, p2, p3], axis=0).astype(BF16), p_mat)
    gs_ref[0] = pieces[0:META_ROWS] + pieces[META_ROWS:2 * META_ROWS] + pieces[2 * META_ROWS:]


def _dispatch(attn, x2d, wo_bf, ln_g, ln_b, wr_split, br_col, tri, *, tm):
    t = x2d.shape[0]
    nt = t // tm
    n_rows = _sorted_rows(tm)
    return pl.pallas_call(
        _dispatch_kernel,
        grid=(nt,),
        in_specs=[pl.BlockSpec((tm, D_MODEL), lambda i: (i, 0)),
                  pl.BlockSpec((tm, D_MODEL), lambda i: (i, 0)),
                  pl.BlockSpec((D_MODEL, D_MODEL), lambda i: (0, 0)),
                  pl.BlockSpec((1, D_MODEL), lambda i: (0, 0)),
                  pl.BlockSpec((1, D_MODEL), lambda i: (0, 0)),
                  pl.BlockSpec((2 * N_EXPERTS, D_MODEL), lambda i: (0, 0)),
                  pl.BlockSpec((N_EXPERTS, 1), lambda i: (0, 0)),
                  pl.BlockSpec((tm, tm), lambda i: (0, 0))],
        out_specs=[pl.BlockSpec((tm, D_MODEL), lambda i: (i, 0)),
                   pl.BlockSpec((1, D_MODEL, n_rows), lambda i: (i, 0, 0)),
                   pl.BlockSpec((1, META_ROWS, n_rows), lambda i: (i, 0, 0)),
                   pl.BlockSpec((1, META_ROWS, tm), lambda i: (i, 0, 0)),
                   pl.BlockSpec((1, META_ROWS, LANES), lambda i: (i, 0, 0))],
        out_shape=[jax.ShapeDtypeStruct((t, D_MODEL), F32),
                   jax.ShapeDtypeStruct((nt, D_MODEL, n_rows), BF16),
                   jax.ShapeDtypeStruct((nt, META_ROWS, n_rows), F32),
                   jax.ShapeDtypeStruct((nt, META_ROWS, tm), F32),
                   jax.ShapeDtypeStruct((nt, META_ROWS, LANES), jnp.int32)],
        compiler_params=_params(("parallel",)),
        name="dispatch",
    )(attn, x2d, wo_bf, ln_g, ln_b, wr_split, br_col, tri)


def _experts_kernel(meta_ref, xs_ref, gs_ref, wg_ref, wu_ref, wd_ref, ys_ref):
    i = pl.program_id(0)
    grp = pl.program_id(1)

    @pl.when(grp == 0)
    def _():
        ys_ref[...] = jnp.zeros_like(ys_ref)

    first = meta_ref[i * META_ROWS + grp]
    n_chunks = meta_ref[i * META_ROWS + N_GROUPS + grp]

    def run(c, width):
        col = pl.multiple_of((first + c) * CHUNK, CHUNK)
        xc = xs_ref[0, :, pl.ds(col, width)]
        gates = gs_ref[0, :, pl.ds(col, width)]
        acc = jnp.zeros((D_MODEL, width), F32)
        for k in range(EXPERTS_PER_GROUP):
            hg = jnp.dot(wg_ref[k], xc, preferred_element_type=F32)
            hu = jnp.dot(wu_ref[k], xc, preferred_element_type=F32)
            h = hg * jax.nn.sigmoid(hg) * hu
            y = jnp.dot(wd_ref[k], h.astype(BF16), preferred_element_type=F32)
            acc = acc + gates[k:k + 1, :] * y
        ys_ref[0, :, pl.ds(col, width)] = acc.astype(BF16)

    def chunk_pair(c2, carry):
        run(2 * c2, 2 * CHUNK)
        return carry

    lax.fori_loop(0, n_chunks >> 1, chunk_pair, 0)

    @pl.when((n_chunks & 1) == 1)
    def _():
        run(n_chunks - 1, CHUNK)


def _experts(meta, xs, gs, wg_t, wu_t, wd_t):
    nt, _, n_rows = xs.shape
    grid_spec = pltpu.PrefetchScalarGridSpec(
        num_scalar_prefetch=1,
        grid=(nt, N_GROUPS),
        in_specs=[pl.BlockSpec((1, D_MODEL, n_rows), lambda i, g, m: (i, 0, 0)),
                  pl.BlockSpec((1, META_ROWS, n_rows), lambda i, g, m: (i, 0, 0)),
                  pl.BlockSpec((EXPERTS_PER_GROUP, D_EXPERT, D_MODEL), lambda i, g, m: (g, 0, 0)),
                  pl.BlockSpec((EXPERTS_PER_GROUP, D_EXPERT, D_MODEL), lambda i, g, m: (g, 0, 0)),
                  pl.BlockSpec((EXPERTS_PER_GROUP, D_MODEL, D_EXPERT), lambda i, g, m: (g, 0, 0))],
        out_specs=pl.BlockSpec((1, D_MODEL, n_rows), lambda i, g, m: (i, 0, 0)),
    )
    return pl.pallas_call(
        _experts_kernel,
        grid_spec=grid_spec,
        out_shape=jax.ShapeDtypeStruct((nt, D_MODEL, n_rows), BF16),
        compiler_params=_params(("parallel", "arbitrary")),
        name="experts",
    )(meta, xs, gs, wg_t, wu_t, wd_t)


def _combine_kernel(ys_ref, pos_ref, x1_ref, g_ref, b_ref, p_ref, wple_ref, wpg_ref, out_ref):
    n_rows = ys_ref.shape[2]
    p_mat = _one_hot_rows(pos_ref[0, 0:1, :], n_rows)
    ffn_t = jnp.dot(ys_ref[0], p_mat, preferred_element_type=F32)
    x2 = _layer_norm(DEEPNORM_ALPHA * x1_ref[...] + ffn_t.T, g_ref[...], b_ref[...])
    ple = jnp.dot(p_ref[...].astype(BF16), wple_ref[...], preferred_element_type=F32)
    gt = jax.nn.sigmoid(jnp.dot(x2.astype(BF16), wpg_ref[...], preferred_element_type=F32))
    out_ref[...] = x2 + ple * gt


def _combine(ys, pos, x1, ln_g, ln_b, p2d, wple_bf, wpg_bf, *, tm):
    t = x1.shape[0]
    n_rows = ys.shape[2]
    return pl.pallas_call(
        _combine_kernel,
        grid=(t // tm,),
        in_specs=[pl.BlockSpec((1, D_MODEL, n_rows), lambda i: (i, 0, 0)),
                  pl.BlockSpec((1, META_ROWS, tm), lambda i: (i, 0, 0)),
                  pl.BlockSpec((tm, D_MODEL), lambda i: (i, 0)),
                  pl.BlockSpec((1, D_MODEL), lambda i: (0, 0)),
                  pl.BlockSpec((1, D_MODEL), lambda i: (0, 0)),
                  pl.BlockSpec((tm, PLE_DIM), lambda i: (i, 0)),
                  pl.BlockSpec((PLE_DIM, D_MODEL), lambda i: (0, 0)),
                  pl.BlockSpec((D_MODEL, D_MODEL), lambda i: (0, 0))],
        out_specs=pl.BlockSpec((tm, D_MODEL), lambda i: (i, 0)),
        out_shape=jax.ShapeDtypeStruct((t, D_MODEL), F32),
        compiler_params=_params(("parallel",)),
        name="combine",
    )(ys, pos, x1, ln_g, ln_b, p2d, wple_bf, wpg_bf)


def _routed_ffn_block(attn2d, x2d, p2d, layer, wts, *, tm):
    x1, xs, gs, pos, meta = _dispatch(attn2d, x2d, wts["w_o"][layer], wts["ln_g"][layer, 0:1],
                                      wts["ln_b"][layer, 0:1], wts["w_router"], wts["b_router"],
                                      wts["tri"], tm=tm)
    ys = _experts(meta[:, :, 0].reshape(-1), xs, gs, wts["w_gate_t"][layer], wts["w_up_t"][layer],
                  wts["w_down_t"][layer])
    return _combine(ys, pos, x1, wts["ln_g"][layer, 1:2], wts["ln_b"][layer, 1:2], p2d,
                    wts["w_ple"][layer], wts["w_ple_gate"][layer], tm=tm)


def _token_tile(t):
    return 512 if t % 512 == 0 else t


def _ffn_block(attn2d, x2d, p2d, layer, wts, *, tm):
    if x2d.shape[0] % MOE_TILE == 0:
        return _routed_ffn_block(attn2d, x2d, p2d, layer, wts, tm=MOE_TILE)
    x1, gates = _post_attn(attn2d, x2d, wts["w_o"][layer], wts["ln_g"][layer, 0:1], wts["ln_b"][layer, 0:1],
                           wts["w_router"], wts["b_router"], tm=tm)
    return _moe(x1, gates, wts["w_gate_t"][layer], wts["w_up_t"][layer], wts["w_down_t"][layer],
                wts["ln_g"][layer, 1:2], wts["ln_b"][layer, 1:2], p2d, wts["w_ple"][layer],
                wts["w_ple_gate"][layer], tm=tm)


def _prompt_trunk(x, p, wts):
    b, s, _ = x.shape
    t = b * s
    tm = _token_tile(s)
    cs = _rope_tables(jnp.arange(s))
    x2d = x.reshape(t, D_MODEL)

    q, k0, v0, khm, vt, km = _proj_rope(x2d, wts["w_qkv"][0], cs, seq=s, tm=tm, attn_layout=True, with_kmean=True)
    kmean = km.reshape(b, s // MOBA_BLOCK, KV_HEADS, HEAD_DIM).transpose(0, 2, 1, 3)
    attn = _moba_prompt(q.reshape(b, s, D_MODEL), khm, vt, kmean)
    x2d = _ffn_block(attn.reshape(t, D_MODEL), x2d, p[0].reshape(t, PLE_DIM), 0, wts, tm=tm)

    q, k1, v1, khm, vt = _proj_rope(x2d, wts["w_qkv"][1], cs, seq=s, tm=tm, attn_layout=True, with_kmean=False)
    attn = _swa_prompt(q.reshape(b, s, D_MODEL), khm, vt, wts["sinks"])
    x2d = _ffn_block(attn.reshape(t, D_MODEL), x2d, p[1].reshape(t, PLE_DIM), 1, wts, tm=tm)
    return (x2d.reshape(b, s, D_MODEL), _heads_last(k0), _heads_last(v0), _heads_last(k1), _heads_last(v1))


def _heads_last(kv_t):
    b, _, s = kv_t.shape
    return jnp.transpose(kv_t.reshape(b, KV_HEADS, HEAD_DIM, s), (0, 3, 1, 2))


def _sample_trunk(x, p, cache_k, cache_v, state_k, state_v, page_table, wts):
    db, tn, _ = x.shape
    t = db * tn
    tm = _token_tile(t)
    pos = PAST_LEN + jnp.arange(tn)
    cs = jnp.tile(_rope_tables(pos), (db, 1))
    x2d = x.reshape(t, D_MODEL)

    q, k0, v0 = _proj_rope(x2d, wts["w_qkv"][0], cs, seq=tn, tm=tm, attn_layout=False, with_kmean=False)
    o = _moba_sample(_block_diag_q(q.reshape(db, tn, D_MODEL)),
                     _pad_new(k0.reshape(db, tn, KV_DIM)), _pad_new(v0.reshape(db, tn, KV_DIM)),
                     _cache_pages_t(cache_k), _cache_pages_t(cache_v), page_table)
    attn = _block_diag_out(o, tn).astype(BF16)
    x2d = _ffn_block(attn.reshape(t, D_MODEL), x2d, p[0].reshape(t, PLE_DIM), 0, wts, tm=tm)

    q, k1, v1 = _proj_rope(x2d, wts["w_qkv"][1], cs, seq=tn, tm=tm, attn_layout=False, with_kmean=False)
    wb = state_k.shape[1]
    sink_rows = jnp.broadcast_to(
        jnp.broadcast_to(wts["sinks"].reshape(KV_HEADS, 1, GROUP), (KV_HEADS, tn, GROUP)).reshape(SAMPLE_ROWS, 1),
        (SAMPLE_ROWS, LANES))
    o = _swa_sample(_block_diag_q(q.reshape(db, tn, D_MODEL)),
                    _tokens_last(state_k), _tokens_last(state_v),
                    _pad_new(k1.reshape(db, tn, KV_DIM)), _pad_new(v1.reshape(db, tn, KV_DIM)), sink_rows)
    attn = _block_diag_out(o, tn).astype(BF16)
    x2d = _ffn_block(attn.reshape(t, D_MODEL), x2d, p[1].reshape(t, PLE_DIM), 1, wts, tm=tm)
    return (x2d.reshape(db, tn, D_MODEL), k0.reshape(db, tn, KV_HEADS, HEAD_DIM), v0.reshape(db, tn, KV_HEADS, HEAD_DIM),
            k1.reshape(db, tn, KV_HEADS, HEAD_DIM), v1.reshape(db, tn, KV_HEADS, HEAD_DIM))


def _split_hi_lo(w):
    hi = w.astype(BF16)
    lo = (w - hi.astype(F32)).astype(BF16)
    return jnp.concatenate([hi, lo], axis=0)


def _tokens_last(x):
    n, tokens = x.shape[0], x.shape[1]
    return jnp.transpose(x, (0, 2, 3, 1)).reshape(n, KV_DIM, tokens)


def _cache_pages_t(cache):
    return _tokens_last(cache.reshape(cache.shape[1:]))


def kernel(x_prompt, x_sample, p_prompt, p_sample, cache_k_a, cache_v_a, state_swa_k, state_swa_v, page_table,
           w_qkv_a, w_o_a, w_kv_s, w_q_b, w_o_b, sinks_b, ln_g, ln_b, w_router, b_router,
           w_exp_gate, w_exp_up, w_exp_down, w_ple, w_ple_gate):
    wts = {
        "w_qkv": jnp.stack([w_qkv_a[0], jnp.concatenate([w_q_b[0], w_kv_s], axis=1)]).astype(BF16),
        "w_o": jnp.stack([w_o_a[0], w_o_b[0]]).astype(BF16),
        "sinks": sinks_b[0].astype(F32),
        "ln_g": ln_g, "ln_b": ln_b,
        "w_router": _split_hi_lo(w_router.astype(F32).T),
        "b_router": b_router.astype(F32).reshape(N_EXPERTS, 1),
        "w_gate_t": jnp.swapaxes(w_exp_gate, 2, 3).astype(BF16),
        "w_up_t": jnp.swapaxes(w_exp_up, 2, 3).astype(BF16),
        "w_down_t": jnp.swapaxes(w_exp_down, 2, 3).astype(BF16),
        "tri": jnp.triu(jnp.ones((MOE_TILE, MOE_TILE), BF16), k=1),
        "w_ple": w_ple.astype(BF16), "w_ple_gate": w_ple_gate.astype(BF16),
    }
    y_p, ka_p, va_p, ks_p, vs_p = _prompt_trunk(x_prompt, p_prompt, wts)
    y_s, ka_s, va_s, ks_s, vs_s = _sample_trunk(x_sample, p_sample, cache_k_a, cache_v_a,
                                                state_swa_k, state_swa_v, page_table, wts)
    wb_p = min(WINDOW, x_prompt.shape[1])
    tn = x_sample.shape[1]
    return (y_p, y_s, ka_p[None], va_p[None], ka_s[None], va_s[None],
            ks_p[:, -wb_p:], vs_p[:, -wb_p:],
            jnp.concatenate([state_swa_k, ks_s], axis=1)[:, tn:],
            jnp.concatenate([state_swa_v, vs_s], axis=1)[:, tn:])
```

```python
import functools

import jax
import jax.numpy as jnp
from jax import lax
from jax.experimental import pallas as pl
from jax.experimental.pallas import tpu as pltpu

D_MODEL = 1024
HEAD_DIM = 64
HALF_DIM = HEAD_DIM // 2
N_HEADS = D_MODEL // HEAD_DIM
KV_HEADS = 4
GROUP = N_HEADS // KV_HEADS
KV_DIM = KV_HEADS * HEAD_DIM
MOBA_BLOCK = 256
BLOCK_SHIFT = MOBA_BLOCK.bit_length() - 1
MOBA_TOPK = 3
WINDOW = 128
PAGE_SIZE = 128
PAST_LEN = 16384
ROPE_THETA = 10000.0
N_EXPERTS = 16
N_GROUPS = 4
EXPERTS_PER_GROUP = N_EXPERTS // N_GROUPS
D_EXPERT = D_MODEL // 2
PLE_DIM = 256
DEPTH = 2
DEEPNORM_ALPHA = (2.0 * DEPTH) ** 0.25
LN_EPS = 1e-5
ATTN_SCALE = HEAD_DIM ** -0.5
LOG2E = 1.4426950408889634
Q_SCALE = ATTN_SCALE * LOG2E

LANES = 128
Q_ROWS = GROUP * MOBA_BLOCK
SAMPLE_ROWS = KV_HEADS * GROUP * 4
NEW_PAD = 8
NEW_ROWS = 128
PAGES_PER_ITEM = 32
CACHE_SLOTS = 4
DENOM_ROWS = 16
VMEM_LIMIT = 56 * 1024 * 1024

F32 = jnp.float32
BF16 = jnp.bfloat16
NEG_INF = float("-inf")
NT_DIMS = (((1,), (1,)), ((), ()))


def _nt_dot(a, b):
    return lax.dot_general(a, b, NT_DIMS, preferred_element_type=F32)


def _params(semantics):
    return pltpu.CompilerParams(dimension_semantics=semantics, vmem_limit_bytes=VMEM_LIMIT)


def _layer_norm(z, g, b):
    mu = jnp.mean(z, axis=-1, keepdims=True)
    zc = z - mu
    var = jnp.mean(zc * zc, axis=-1, keepdims=True)
    return zc * lax.rsqrt(var + LN_EPS) * g + b


def _rope_tables(pos):
    inv_freq = ROPE_THETA ** (-jnp.arange(HALF_DIM, dtype=F32) / HALF_DIM)
    ang = pos.astype(F32)[:, None] * inv_freq[None, :]
    cos = jnp.cos(ang)
    sin = jnp.sin(ang)
    cos_t = jnp.tile(cos, (1, 4))
    sin_t = jnp.tile(jnp.concatenate([-sin, sin], axis=1), (1, 2))
    return jnp.concatenate([cos_t, sin_t], axis=1)


def _proj_rope_kernel(x_ref, w_ref, cs_ref, q_ref, k_ref, v_ref, *rest, attn_layout, with_kmean):
    x = x_ref[...].astype(BF16)
    qkv = jnp.dot(x, w_ref[...], preferred_element_type=F32)
    tm = x.shape[0]
    cos = cs_ref[:, :LANES]
    sin = cs_ref[:, LANES:]
    lane = lax.broadcasted_iota(jnp.int32, (tm, LANES), 1)
    first_half = (lane & HALF_DIM) == 0

    def rope(blk):
        partner = jnp.where(first_half, pltpu.roll(blk, LANES - HALF_DIM, 1), pltpu.roll(blk, HALF_DIM, 1))
        return blk * cos + partner * sin

    for j in range(D_MODEL // LANES):
        q_ref[:, j * LANES:(j + 1) * LANES] = (rope(qkv[:, j * LANES:(j + 1) * LANES]) * Q_SCALE).astype(BF16)
    k = jnp.concatenate([rope(qkv[:, D_MODEL + j * LANES:D_MODEL + (j + 1) * LANES])
                         for j in range(KV_DIM // LANES)], axis=1)
    v = qkv[:, D_MODEL + KV_DIM:]
    if attn_layout:
        khm_ref, vt_ref = rest[0], rest[1]
        v_t = v.T
        k_ref[0] = k.T
        v_ref[0] = v_t
        for h in range(KV_HEADS):
            khm_ref[0, h] = k[:, h * HEAD_DIM:(h + 1) * HEAD_DIM].astype(BF16)
        vt_ref[0] = v_t.astype(BF16)
    else:
        k_ref[...] = k
        v_ref[...] = v
    if with_kmean:
        km_ref = rest[2]
        nb = tm // MOBA_BLOCK
        km_ref[0] = jnp.sum(k.reshape(nb, MOBA_BLOCK, KV_DIM), axis=1) * (1.0 / MOBA_BLOCK)


def _proj_rope(x2d, w_bf, cs, *, seq, tm, attn_layout, with_kmean):
    t = x2d.shape[0]
    nt = t // tm
    tab_tiles = cs.shape[0] // tm
    out_shape = [jax.ShapeDtypeStruct((t, D_MODEL), BF16)]
    out_specs = [pl.BlockSpec((tm, D_MODEL), lambda i: (i, 0))]
    if attn_layout:
        nseq = t // seq
        tiles_per_seq = seq // tm
        seq_tile = pl.BlockSpec((1, KV_DIM, tm), lambda i: (i // tiles_per_seq, 0, i % tiles_per_seq))
        out_shape += [jax.ShapeDtypeStruct((nseq, KV_DIM, seq), F32),
                      jax.ShapeDtypeStruct((nseq, KV_DIM, seq), F32),
                      jax.ShapeDtypeStruct((nseq, KV_HEADS, seq, HEAD_DIM), BF16),
                      jax.ShapeDtypeStruct((nseq, KV_DIM, seq), BF16)]
        out_specs += [seq_tile, seq_tile,
                      pl.BlockSpec((1, KV_HEADS, tm, HEAD_DIM),
                                   lambda i: (i // tiles_per_seq, 0, i % tiles_per_seq, 0)),
                      seq_tile]
    else:
        out_shape += [jax.ShapeDtypeStruct((t, KV_DIM), F32), jax.ShapeDtypeStruct((t, KV_DIM), F32)]
        out_specs += [pl.BlockSpec((tm, KV_DIM), lambda i: (i, 0)), pl.BlockSpec((tm, KV_DIM), lambda i: (i, 0))]
    if with_kmean:
        out_shape.append(jax.ShapeDtypeStruct((nt, tm // MOBA_BLOCK, KV_DIM), F32))
        out_specs.append(pl.BlockSpec((1, tm // MOBA_BLOCK, KV_DIM), lambda i: (i, 0, 0)))
    return pl.pallas_call(
        functools.partial(_proj_rope_kernel, attn_layout=attn_layout, with_kmean=with_kmean),
        grid=(nt,),
        in_specs=[pl.BlockSpec((tm, D_MODEL), lambda i: (i, 0)),
                  pl.BlockSpec(w_bf.shape, lambda i: (0, 0)),
                  pl.BlockSpec((tm, 2 * LANES), lambda i: (i % tab_tiles, 0))],
        out_specs=out_specs,
        out_shape=out_shape,
        compiler_params=_params(("parallel",)),
        name="proj_rope",
    )(x2d, w_bf, cs)


def _stack_heads(qb):
    return jnp.concatenate([qb[:, g * HEAD_DIM:(g + 1) * HEAD_DIM] for g in range(GROUP)], axis=0)


def _unstack_out(out_t):
    o2 = jnp.concatenate([out_t[:, g * MOBA_BLOCK:(g + 1) * MOBA_BLOCK] for g in range(GROUP)], axis=0)
    return o2.T.astype(BF16)


def _lane_chunks():
    return [slice(c * LANES, (c + 1) * LANES) for c in range(Q_ROWS // LANES)]


def _moba_prompt_kernel(q_ref, k_ref, vt_ref, km_ref, o_ref,
                        sel_ref, m_ref, a_ref, p_ref, acc_ref, s_ref, bm_ref):
    n = pl.program_id(2)
    nblk = km_ref.shape[2]
    qrows = _stack_heads(q_ref[0])

    gate = _nt_dot(km_ref[0, 0].astype(BF16), qrows)
    jio = lax.broadcasted_iota(jnp.int32, (nblk, Q_ROWS), 0)
    past = jio < n
    g = jnp.where(past, gate, NEG_INF)
    sel = jnp.zeros((nblk, Q_ROWS), jnp.bool_)
    for _ in range(MOBA_TOPK):
        mx = jnp.max(g, axis=0, keepdims=True)
        idx = jnp.min(jnp.where(g == mx, jio, nblk), axis=0, keepdims=True)
        hit = jio == idx
        sel = sel | hit
        g = jnp.where(hit, NEG_INF, g)
    sel_ref[...] = (sel & past).astype(F32)

    own = pl.multiple_of(n * MOBA_BLOCK, MOBA_BLOCK)
    k_own = k_ref[0, 0, pl.ds(own, MOBA_BLOCK), :]
    kio = lax.broadcasted_iota(jnp.int32, (MOBA_BLOCK, LANES), 0)
    lio = lax.broadcasted_iota(jnp.int32, (MOBA_BLOCK, LANES), 1)
    for ci, cs in enumerate(_lane_chunks()):
        s = _nt_dot(k_own, qrows[cs, :])
        qio = (lio + ci * LANES) & (MOBA_BLOCK - 1)
        s = jnp.where(kio <= qio, s, NEG_INF)
        m0 = jnp.max(s, axis=0, keepdims=True)
        p = jnp.exp2(s - m0)
        m_ref[:, cs] = m0
        p_ref[:, cs] = p.astype(BF16)
    acc_ref[...] = jnp.zeros_like(acc_ref)
    ones_rows = jnp.ones((DENOM_ROWS, MOBA_BLOCK), BF16)

    def pv(blk):
        off = pl.multiple_of(blk * MOBA_BLOCK, MOBA_BLOCK)
        v_ext = jnp.concatenate([vt_ref[0, :, pl.ds(off, MOBA_BLOCK)], ones_rows], axis=0)
        return jnp.dot(v_ext, p_ref[...], preferred_element_type=F32)

    def scores(j, slot):
        off = pl.multiple_of(j * MOBA_BLOCK, MOBA_BLOCK)
        k_j = k_ref[0, 0, pl.ds(off, MOBA_BLOCK), :]
        for cs in _lane_chunks():
            sj = _nt_dot(k_j, qrows[cs, :])
            s_ref[slot, :, cs] = sj
            bm_ref[slot, :, cs] = jnp.max(sj, axis=0, keepdims=True)

    def softmax(j, slot):
        pv_prev = pv(jnp.where(j == 0, n, j - 1))
        sel_j = sel_ref[pl.ds(j, 1), :]
        for cs in _lane_chunks():
            picked = sel_j[:, cs] > 0.0
            m_old = m_ref[:, cs]
            m_new = jnp.where(picked, jnp.maximum(m_old, bm_ref[slot, :, cs]), m_old)
            a = jnp.exp2(m_old - m_new)
            pj = jnp.exp2(s_ref[slot, :, cs] - jnp.where(picked, m_new, jnp.inf))
            m_ref[:, cs] = m_new
            a_ref[:, cs] = a
            p_ref[:, cs] = pj.astype(BF16)
        acc_ref[...] = a_ref[...] * (acc_ref[...] + pv_prev)

    scores(0, 0)

    def block_pair(i, carry):
        j = 2 * i
        scores(j + 1, 1)
        softmax(j, 0)
        scores(jnp.minimum(j + 2, n - 1), 0)
        softmax(j + 1, 1)
        return carry

    lax.fori_loop(0, n >> 1, block_pair, 0)

    @pl.when((n & 1) == 1)
    def _():
        softmax(n - 1, 0)

    last = jnp.where(n == 0, 0, n - 1)
    total = acc_ref[...] + pv(last)
    o_ref[0] = _unstack_out(total[:HEAD_DIM] / total[HEAD_DIM:HEAD_DIM + 1])


def _moba_prompt(q, khm, vt, kmean):
    b, s, _ = q.shape
    nblk = s // MOBA_BLOCK
    return pl.pallas_call(
        _moba_prompt_kernel,
        grid=(b, KV_HEADS, nblk),
        in_specs=[pl.BlockSpec((1, MOBA_BLOCK, KV_DIM), lambda bi, h, n: (bi, n, h)),
                  pl.BlockSpec((1, 1, s, HEAD_DIM), lambda bi, h, n: (bi, h, 0, 0)),
                  pl.BlockSpec((1, HEAD_DIM, s), lambda bi, h, n: (bi, h, 0)),
                  pl.BlockSpec((1, 1, nblk, HEAD_DIM), lambda bi, h, n: (bi, h, 0, 0))],
        out_specs=pl.BlockSpec((1, MOBA_BLOCK, KV_DIM), lambda bi, h, n: (bi, n, h)),
        out_shape=jax.ShapeDtypeStruct((b, s, D_MODEL), BF16),
        scratch_shapes=[pltpu.VMEM((nblk, Q_ROWS), F32),
                        pltpu.VMEM((1, Q_ROWS), F32),
                        pltpu.VMEM((1, Q_ROWS), F32),
                        pltpu.VMEM((MOBA_BLOCK, Q_ROWS), BF16),
                        pltpu.VMEM((HEAD_DIM + DENOM_ROWS, Q_ROWS), F32),
                        pltpu.VMEM((2, MOBA_BLOCK, Q_ROWS), F32),
                        pltpu.VMEM((2, 1, Q_ROWS), F32)],
        compiler_params=_params(("parallel", "parallel", "arbitrary")),
        name="moba_prompt",
    )(q, khm, vt, kmean)


SWA_KEYS = MOBA_BLOCK + WINDOW


def _swa_prompt_kernel(sink_ref, q_ref, k_ref, vt_ref, o_ref, d_ref, p_ref, bias_ref, s_ref, m_ref):
    h = pl.program_id(1)
    n = pl.program_id(2)
    blocks = q_ref.shape[1] // MOBA_BLOCK

    @pl.when(n == 0)
    def _():
        ki = lax.broadcasted_iota(jnp.int32, (SWA_KEYS, LANES), 0)
        lio = lax.broadcasted_iota(jnp.int32, (SWA_KEYS, LANES), 1)
        for ci, cs in enumerate(_lane_chunks()):
            rel = ((lio + ci * LANES) & (MOBA_BLOCK - 1)) - ki
            for variant, shift in ((0, 0), (1, WINDOW)):
                diff = rel + shift
                bias_ref[variant, :, cs] = jnp.where((diff >= 0) & (diff < WINDOW), 0.0, NEG_INF)

    for u in range(blocks):
        blk = n * blocks + u
        rows = slice(u * MOBA_BLOCK, (u + 1) * MOBA_BLOCK)
        qrows = _stack_heads(q_ref[0, rows, :])
        start = pl.multiple_of(jnp.maximum(blk * MOBA_BLOCK - WINDOW, 0), WINDOW)
        variant = jnp.minimum(blk, 1) if u == 0 else 1
        k_win = k_ref[0, 0, pl.ds(start, SWA_KEYS), :]
        for ci, cs in enumerate(_lane_chunks()):
            s = _nt_dot(k_win, qrows[cs, :]) + bias_ref[variant, :, cs]
            sink = sink_ref[h * GROUP + (ci * LANES) // MOBA_BLOCK] * LOG2E
            m = jnp.maximum(jnp.max(s, axis=0, keepdims=True), sink)
            s_ref[u, :, cs] = s
            m_ref[u, :, cs] = m
            d_ref[u, :, cs] = jnp.exp2(sink - m)
        for cs in _lane_chunks():
            p_ref[u, :, cs] = jnp.exp2(s_ref[u, :, cs] - m_ref[u, :, cs]).astype(BF16)
        v_ext = jnp.concatenate([vt_ref[0, :, pl.ds(start, SWA_KEYS)], jnp.ones((DENOM_ROWS, SWA_KEYS), BF16)],
                                axis=0)
        total = jnp.dot(v_ext, p_ref[u], preferred_element_type=F32)
        o_ref[0, rows, :] = _unstack_out(total[:HEAD_DIM] / (total[HEAD_DIM:HEAD_DIM + 1] + d_ref[u]))


def _swa_prompt(q, khm, vt, sinks):
    b, s, _ = q.shape
    nblk = s // MOBA_BLOCK
    blocks = 2 if nblk % 2 == 0 else 1
    grid_spec = pltpu.PrefetchScalarGridSpec(
        num_scalar_prefetch=1,
        grid=(b, KV_HEADS, nblk // blocks),
        in_specs=[pl.BlockSpec((1, blocks * MOBA_BLOCK, KV_DIM), lambda bi, h, n, sk: (bi, n, h)),
                  pl.BlockSpec((1, 1, s, HEAD_DIM), lambda bi, h, n, sk: (bi, h, 0, 0)),
                  pl.BlockSpec((1, HEAD_DIM, s), lambda bi, h, n, sk: (bi, h, 0))],
        out_specs=pl.BlockSpec((1, blocks * MOBA_BLOCK, KV_DIM), lambda bi, h, n, sk: (bi, n, h)),
        scratch_shapes=[pltpu.VMEM((blocks, 1, Q_ROWS), F32),
                        pltpu.VMEM((blocks, SWA_KEYS, Q_ROWS), BF16),
                        pltpu.VMEM((2, SWA_KEYS, Q_ROWS), F32),
                        pltpu.VMEM((blocks, SWA_KEYS, Q_ROWS), F32),
                        pltpu.VMEM((blocks, 1, Q_ROWS), F32)],
    )
    return pl.pallas_call(
        _swa_prompt_kernel,
        grid_spec=grid_spec,
        out_shape=jax.ShapeDtypeStruct((b, s, D_MODEL), BF16),
        compiler_params=_params(("parallel", "parallel", "arbitrary")),
        name="swa_prompt",
    )(sinks, q, khm, vt)


def _block_diag_q(q):
    db, t, _ = q.shape
    q5 = q.reshape(db, t, KV_HEADS, GROUP, HEAD_DIM).transpose(0, 2, 1, 3, 4)
    q5 = q5.reshape(db, KV_HEADS, t * GROUP, HEAD_DIM)
    eye = jnp.eye(KV_HEADS, dtype=q.dtype)
    return jnp.einsum("ikrd,kc->ikrcd", q5, eye).reshape(db, KV_HEADS * t * GROUP, KV_DIM)


def _block_diag_out(o, t):
    db = o.shape[0]
    o5 = o.reshape(db, KV_HEADS, t * GROUP, KV_HEADS, HEAD_DIM)
    od = jnp.stack([o5[:, h, :, h, :] for h in range(KV_HEADS)], axis=1)
    return od.reshape(db, KV_HEADS, t, GROUP, HEAD_DIM).transpose(0, 2, 1, 3, 4).reshape(db, t, D_MODEL)


def _pad_new(x):
    return jnp.pad(x, ((0, 0), (0, NEW_PAD - x.shape[1]), (0, 0)))


def _row_token(shape):
    return (lax.broadcasted_iota(jnp.int32, shape, 0) >> 2) & 3


def _new_scores(qbd, knew):
    pad = jnp.zeros((NEW_ROWS - NEW_PAD, KV_DIM), BF16)
    s_new = _nt_dot(qbd, jnp.concatenate([knew.astype(BF16), pad], axis=0))
    tcol = lax.broadcasted_iota(jnp.int32, s_new.shape, 1)
    return s_new, tcol <= _row_token(s_new.shape)


def _new_token_part(e_new, v_new):
    out = e_new[:, 0:1] * v_new[0:1, :]
    for t in range(1, 4):
        out = out + e_new[:, t:t + 1] * v_new[t:t + 1, :]
    return out


def _moba_sample_kernel(pt_ref, qbd_ref, knew_ref, vnew_ref, ck_ref, cv_ref, o_ref,
                        buf, sem, s_all, p_all, gate_s, acc_s):
    i = pl.program_id(0)
    ns = pl.num_programs(0)
    n_pages = pt_ref.shape[1]
    k_items = n_pages // PAGES_PER_ITEM
    n_items = 2 * k_items
    item_cols = PAGES_PER_ITEM * PAGE_SIZE
    blocks_per_item = item_cols // MOBA_BLOCK
    n_blocks = k_items * blocks_per_item
    nb_pad = gate_s.shape[1]
    qbd = qbd_ref[0]

    def page_copy(src, page, slot, pg):
        return pltpu.make_async_copy(src.at[page], buf.at[slot, :, pl.ds(pg * PAGE_SIZE, PAGE_SIZE)],
                                     sem.at[slot])

    def start_item(sample, it, slot):
        src = ck_ref if it < k_items else cv_ref
        c = it % k_items
        for pg in range(PAGES_PER_ITEM):
            page_copy(src, pt_ref[sample, c * PAGES_PER_ITEM + pg], slot, pg).start()

    def wait_item(slot):
        for pg in range(PAGES_PER_ITEM):
            page_copy(ck_ref, 0, slot, pg).wait()

    ahead = CACHE_SLOTS - 1

    @pl.when(i == 0)
    def _():
        for it in range(ahead):
            start_item(0, it, it)

    gate_s[...] = jnp.zeros_like(gate_s)
    glane = lax.broadcasted_iota(jnp.int32, (SAMPLE_ROWS, nb_pad), 1)
    own_out = None
    l_tot = None
    for it in range(n_items):
        slot = it % CACHE_SLOTS
        nxt = it + ahead
        if nxt < n_items:
            start_item(i, nxt, nxt % CACHE_SLOTS)
        else:
            @pl.when(i + 1 < ns)
            def _(nxt=nxt):
                start_item(i + 1, nxt - n_items, nxt % CACHE_SLOTS)
        wait_item(slot)

        if it < k_items:
            s = jnp.dot(qbd, buf[slot].astype(BF16), preferred_element_type=F32)
            s_all[:, it * item_cols:(it + 1) * item_cols] = s
            gate = gate_s[...]
            for bi in range(blocks_per_item):
                gate = jnp.where(glane == it * blocks_per_item + bi,
                                 jnp.sum(s[:, bi * MOBA_BLOCK:(bi + 1) * MOBA_BLOCK], axis=1, keepdims=True), gate)
            gate_s[...] = gate

        if it == k_items - 1:
            g = jnp.where(glane < n_blocks, gate_s[...], NEG_INF)
            sel = jnp.zeros((SAMPLE_ROWS, nb_pad), jnp.bool_)
            for _ in range(min(MOBA_TOPK, n_blocks)):
                mx = jnp.max(g, axis=1, keepdims=True)
                idx = jnp.min(jnp.where(g == mx, glane, nb_pad), axis=1, keepdims=True)
                hit = glane == idx
                sel = sel | hit
                g = jnp.where(hit, NEG_INF, g)
            sel_bf = sel.astype(BF16)

            s_new, new_ok = _new_scores(qbd, knew_ref[0])
            m = jnp.max(jnp.where(new_ok, s_new, NEG_INF), axis=1, keepdims=True)

            def expand(c):
                blk = lax.broadcasted_iota(jnp.int32, (nb_pad, item_cols), 0)
                key_blk = lax.broadcasted_iota(jnp.int32, (nb_pad, item_cols), 1) >> BLOCK_SHIFT
                ex = (blk == key_blk + c * blocks_per_item).astype(BF16)
                return jnp.dot(sel_bf, ex, preferred_element_type=F32) > 0.5

            for c in range(k_items):
                sc = jnp.where(expand(c), s_all[:, c * item_cols:(c + 1) * item_cols], NEG_INF)
                s_all[:, c * item_cols:(c + 1) * item_cols] = sc
                m = jnp.maximum(m, jnp.max(sc, axis=1, keepdims=True))
            e_new = jnp.where(new_ok, jnp.exp2(s_new - m), 0.0)
            l_tot = jnp.sum(e_new, axis=1, keepdims=True)
            for c in range(k_items):
                pc = jnp.exp2(s_all[:, c * item_cols:(c + 1) * item_cols] - m)
                l_tot = l_tot + jnp.sum(pc, axis=1, keepdims=True)
                p_all[:, c * item_cols:(c + 1) * item_cols] = pc.astype(BF16)
            own_out = _new_token_part(e_new, vnew_ref[0])
            acc_s[...] = jnp.zeros_like(acc_s)

        if it >= k_items:
            c = it - k_items
            acc_s[...] += _nt_dot(p_all[:, c * item_cols:(c + 1) * item_cols], buf[slot].astype(BF16))

    o_ref[0] = (acc_s[...] + own_out) / l_tot


def _moba_sample(qbd, k_new, v_new, cache_kt, cache_vt, page_table):
    db = qbd.shape[0]
    n_pages = page_table.shape[1]
    past = n_pages * PAGE_SIZE
    item_cols = PAGES_PER_ITEM * PAGE_SIZE
    nb_pad = -(-(past // MOBA_BLOCK) // LANES) * LANES
    assert n_pages % PAGES_PER_ITEM == 0 and (2 * n_pages // PAGES_PER_ITEM) % CACHE_SLOTS == 0
    grid_spec = pltpu.PrefetchScalarGridSpec(
        num_scalar_prefetch=1,
        grid=(db,),
        in_specs=[pl.BlockSpec((1, SAMPLE_ROWS, KV_DIM), lambda i, pt: (i, 0, 0)),
                  pl.BlockSpec((1, NEW_PAD, KV_DIM), lambda i, pt: (i, 0, 0)),
                  pl.BlockSpec((1, NEW_PAD, KV_DIM), lambda i, pt: (i, 0, 0)),
                  pl.BlockSpec(memory_space=pl.ANY),
                  pl.BlockSpec(memory_space=pl.ANY)],
        out_specs=pl.BlockSpec((1, SAMPLE_ROWS, KV_DIM), lambda i, pt: (i, 0, 0)),
        scratch_shapes=[pltpu.VMEM((CACHE_SLOTS, KV_DIM, item_cols), F32),
                        pltpu.SemaphoreType.DMA((CACHE_SLOTS,)),
                        pltpu.VMEM((SAMPLE_ROWS, past), F32),
                        pltpu.VMEM((SAMPLE_ROWS, past), BF16),
                        pltpu.VMEM((SAMPLE_ROWS, nb_pad), F32),
                        pltpu.VMEM((SAMPLE_ROWS, KV_DIM), F32)],
    )
    return pl.pallas_call(
        _moba_sample_kernel,
        grid_spec=grid_spec,
        out_shape=jax.ShapeDtypeStruct((db, SAMPLE_ROWS, KV_DIM), F32),
        compiler_params=_params(("arbitrary",)),
        name="moba_sample",
    )(page_table, qbd, k_new, v_new, cache_kt, cache_vt)


def _swa_sample_kernel(qbd_ref, kbuf_ref, vbuf_ref, knew_ref, vnew_ref, sink_ref, o_ref):
    wb = kbuf_ref.shape[2]
    bcol = lax.broadcasted_iota(jnp.int32, (SAMPLE_ROWS, wb), 1)
    bdiff = wb + _row_token((SAMPLE_ROWS, wb)) - bcol
    buf_ok = (bdiff >= 0) & (bdiff < WINDOW)
    sink = sink_ref[:, 0:1] * LOG2E
    for b in range(qbd_ref.shape[0]):
        qbd = qbd_ref[b]
        s_buf = jnp.dot(qbd, kbuf_ref[b].astype(BF16), preferred_element_type=F32)
        s_new, new_ok = _new_scores(qbd, knew_ref[b])
        m = jnp.maximum(jnp.max(jnp.where(buf_ok, s_buf, NEG_INF), axis=1, keepdims=True),
                        jnp.max(jnp.where(new_ok, s_new, NEG_INF), axis=1, keepdims=True))
        m = jnp.maximum(m, sink)
        e_buf = jnp.where(buf_ok, jnp.exp2(s_buf - m), 0.0)
        e_new = jnp.where(new_ok, jnp.exp2(s_new - m), 0.0)
        denom = (jnp.sum(e_buf, axis=1, keepdims=True) + jnp.sum(e_new, axis=1, keepdims=True)
                 + jnp.exp2(sink - m))
        out = _nt_dot(e_buf.astype(BF16), vbuf_ref[b].astype(BF16))
        o_ref[b] = (out + _new_token_part(e_new, vnew_ref[b])) / denom


def _swa_sample(qbd, k_buf_t, v_buf_t, k_new, v_new, sink_rows):
    db, _, wb = k_buf_t.shape
    sb = next(c for c in (8, 4, 2, 1) if db % c == 0)
    return pl.pallas_call(
        _swa_sample_kernel,
        grid=(db // sb,),
        in_specs=[pl.BlockSpec((sb, SAMPLE_ROWS, KV_DIM), lambda i: (i, 0, 0)),
                  pl.BlockSpec((sb, KV_DIM, wb), lambda i: (i, 0, 0)),
                  pl.BlockSpec((sb, KV_DIM, wb), lambda i: (i, 0, 0)),
                  pl.BlockSpec((sb, NEW_PAD, KV_DIM), lambda i: (i, 0, 0)),
                  pl.BlockSpec((sb, NEW_PAD, KV_DIM), lambda i: (i, 0, 0)),
                  pl.BlockSpec((SAMPLE_ROWS, LANES), lambda i: (0, 0))],
        out_specs=pl.BlockSpec((sb, SAMPLE_ROWS, KV_DIM), lambda i: (i, 0, 0)),
        out_shape=jax.ShapeDtypeStruct((db, SAMPLE_ROWS, KV_DIM), F32),
        compiler_params=_params(("parallel",)),
        name="swa_sample",
    )(qbd, k_buf_t, v_buf_t, k_new, v_new, sink_rows)


def _first_of(vals, m):
    idx = jnp.full(m.shape, EXPERTS_PER_GROUP - 1, jnp.int32)
    for k in range(EXPERTS_PER_GROUP - 2, -1, -1):
        idx = jnp.where(vals[k] == m, k, idx)
    return idx


def _route_picks(aff, bias):
    biased = aff + bias
    best = e1 = e2 = None
    for grp in range(N_GROUPS):
        v = [biased[grp * EXPERTS_PER_GROUP + k:grp * EXPERTS_PER_GROUP + k + 1, :] for k in range(EXPERTS_PER_GROUP)]
        m1 = jnp.maximum(jnp.maximum(v[0], v[1]), jnp.maximum(v[2], v[3]))
        i1 = _first_of(v, m1)
        v2 = [jnp.where(i1 == k, NEG_INF, v[k]) for k in range(EXPERTS_PER_GROUP)]
        m2 = jnp.maximum(jnp.maximum(v2[0], v2[1]), jnp.maximum(v2[2], v2[3]))
        i2 = _first_of(v2, m2)
        gs = m1 + m2
        if grp == 0:
            best, e1, e2 = gs, i1, i2
        else:
            better = gs > best
            best = jnp.where(better, gs, best)
            e1 = jnp.where(better, grp * EXPERTS_PER_GROUP + i1, e1)
            e2 = jnp.where(better, grp * EXPERTS_PER_GROUP + i2, e2)
    eio = lax.broadcasted_iota(jnp.int32, aff.shape, 0)
    sel1 = eio == e1
    sel2 = eio == e2
    a1 = jnp.sum(jnp.where(sel1, aff, 0.0), axis=0, keepdims=True)
    a2 = jnp.sum(jnp.where(sel2, aff, 0.0), axis=0, keepdims=True)
    den = a1 + a2
    return e1, e2, a1 / den, a2 / den


def _route_t(aff, bias):
    e1, e2, g1, g2 = _route_picks(aff, bias)
    eio = lax.broadcasted_iota(jnp.int32, aff.shape, 0)
    return jnp.where(eio == e1, g1, 0.0) + jnp.where(eio == e2, g2, 0.0)


def _router_affinity(x1, wr_ref):
    x_hi = x1.astype(BF16)
    x_lo = (x1 - x_hi.astype(F32)).astype(BF16)
    w2 = wr_ref[...]
    t_hi = _nt_dot(w2, x_hi)
    t_lo = _nt_dot(w2[:N_EXPERTS], x_lo)
    return jax.nn.sigmoid(t_hi[:N_EXPERTS] + t_hi[N_EXPERTS:] + t_lo)


def _post_attn_kernel(a_ref, x_ref, wo_ref, g_ref, b_ref, wr_ref, br_ref, x1_ref, gates_ref):
    o = jnp.dot(a_ref[...], wo_ref[...], preferred_element_type=F32)
    x1 = _layer_norm(DEEPNORM_ALPHA * x_ref[...] + o, g_ref[...], b_ref[...])
    x1_ref[...] = x1
    gates_t = _route_t(_router_affinity(x1, wr_ref), br_ref[...])
    tm = x1.shape[0]
    gates_ref[...] = jnp.concatenate([gates_t, jnp.zeros((LANES - N_EXPERTS, tm), F32)], axis=0).T


def _post_attn(attn, x2d, wo_bf, ln_g, ln_b, wr_split, br_col, *, tm):
    t = x2d.shape[0]
    return pl.pallas_call(
        _post_attn_kernel,
        grid=(t // tm,),
        in_specs=[pl.BlockSpec((tm, D_MODEL), lambda i: (i, 0)),
                  pl.BlockSpec((tm, D_MODEL), lambda i: (i, 0)),
                  pl.BlockSpec((D_MODEL, D_MODEL), lambda i: (0, 0)),
                  pl.BlockSpec((1, D_MODEL), lambda i: (0, 0)),
                  pl.BlockSpec((1, D_MODEL), lambda i: (0, 0)),
                  pl.BlockSpec((2 * N_EXPERTS, D_MODEL), lambda i: (0, 0)),
                  pl.BlockSpec((N_EXPERTS, 1), lambda i: (0, 0))],
        out_specs=[pl.BlockSpec((tm, D_MODEL), lambda i: (i, 0)),
                   pl.BlockSpec((tm, LANES), lambda i: (i, 0))],
        out_shape=[jax.ShapeDtypeStruct((t, D_MODEL), F32),
                   jax.ShapeDtypeStruct((t, LANES), F32)],
        compiler_params=_params(("parallel",)),
        name="post_attn",
    )(attn, x2d, wo_bf, ln_g, ln_b, wr_split, br_col)


def _moe_kernel(x1_ref, gates_ref, wg_ref, wu_ref, wd_ref, g_ref, b_ref, p_ref, wple_ref, wpg_ref,
                out_ref, xb_ref, acc_ref):
    e = pl.program_id(1)

    @pl.when(e == 0)
    def _():
        xb_ref[...] = x1_ref[...].astype(BF16)
        acc_ref[...] = jnp.zeros_like(acc_ref)

    xb = xb_ref[...]
    hg = _nt_dot(xb, wg_ref[0])
    hu = _nt_dot(xb, wu_ref[0])
    h = hg * jax.nn.sigmoid(hg) * hu
    y = _nt_dot(h.astype(BF16), wd_ref[0])
    gates = gates_ref[...]
    lane = lax.broadcasted_iota(jnp.int32, gates.shape, 1)
    gate_e = jnp.sum(jnp.where(lane == e, gates, 0.0), axis=1, keepdims=True)
    acc_ref[...] += gate_e * y

    @pl.when(e == pl.num_programs(1) - 1)
    def _():
        x2 = _layer_norm(DEEPNORM_ALPHA * x1_ref[...] + acc_ref[...], g_ref[...], b_ref[...])
        ple = jnp.dot(p_ref[...].astype(BF16), wple_ref[...], preferred_element_type=F32)
        gt = jax.nn.sigmoid(jnp.dot(x2.astype(BF16), wpg_ref[...], preferred_element_type=F32))
        out_ref[...] = x2 + ple * gt


def _moe(x1, gates, wg_bf, wu_bf, wd_bf, ln_g, ln_b, p2d, wple_bf, wpg_bf, *, tm):
    t = x1.shape[0]
    return pl.pallas_call(
        _moe_kernel,
        grid=(t // tm, N_EXPERTS),
        in_specs=[pl.BlockSpec((tm, D_MODEL), lambda i, e: (i, 0)),
                  pl.BlockSpec((tm, LANES), lambda i, e: (i, 0)),
                  pl.BlockSpec((1, D_EXPERT, D_MODEL), lambda i, e: (e, 0, 0)),
                  pl.BlockSpec((1, D_EXPERT, D_MODEL), lambda i, e: (e, 0, 0)),
                  pl.BlockSpec((1, D_MODEL, D_EXPERT), lambda i, e: (e, 0, 0)),
                  pl.BlockSpec((1, D_MODEL), lambda i, e: (0, 0)),
                  pl.BlockSpec((1, D_MODEL), lambda i, e: (0, 0)),
                  pl.BlockSpec((tm, PLE_DIM), lambda i, e: (i, 0)),
                  pl.BlockSpec((PLE_DIM, D_MODEL), lambda i, e: (0, 0)),
                  pl.BlockSpec((D_MODEL, D_MODEL), lambda i, e: (0, 0))],
        out_specs=pl.BlockSpec((tm, D_MODEL), lambda i, e: (i, 0)),
        out_shape=jax.ShapeDtypeStruct((t, D_MODEL), F32),
        scratch_shapes=[pltpu.VMEM((tm, D_MODEL), BF16),
                        pltpu.VMEM((tm, D_MODEL), F32)],
        compiler_params=_params(("parallel", "arbitrary")),
        name="moe",
    )(x1, gates, wg_bf, wu_bf, wd_bf, ln_g, ln_b, p2d, wple_bf, wpg_bf)


MOE_TILE = 1024
CHUNK = LANES
META_ROWS = 8


def _sorted_rows(tm):
    return tm + N_GROUPS * CHUNK


def _one_hot_rows(pos_row, n_rows):
    tm = pos_row.shape[1]
    jio = lax.broadcasted_iota(jnp.int32, (n_rows, tm), 0)
    return (jio == pos_row.astype(jnp.int32)).astype(BF16)


def _dispatch_kernel(a_ref, x_ref, wo_ref, g_ref, b_ref, wr_ref, br_ref, tri_ref,
                     x1_ref, xs_ref, gs_ref, pos_ref, meta_ref):
    o = jnp.dot(a_ref[...], wo_ref[...], preferred_element_type=F32)
    x1 = _layer_norm(DEEPNORM_ALPHA * x_ref[...] + o, g_ref[...], b_ref[...])
    x1_ref[...] = x1
    tm = x1.shape[0]
    n_rows = xs_ref.shape[2]
    e1, e2, g1, g2 = _route_picks(_router_affinity(x1, wr_ref), br_ref[...])
    gid = e1 >> 2
    k1 = e1 & (EXPERTS_PER_GROUP - 1)
    k2 = e2 & (EXPERTS_PER_GROUP - 1)

    gio = lax.broadcasted_iota(jnp.int32, (META_ROWS, tm), 0)
    oh = gio == gid
    ohf = oh.astype(F32)
    rank = jnp.dot(oh.astype(BF16), tri_ref[...], preferred_element_type=F32)
    count = jnp.sum(ohf, axis=1, keepdims=True).astype(jnp.int32)
    nch = (count + (CHUNK - 1)) >> (CHUNK.bit_length() - 1)
    starts = [jnp.zeros((1, 1), jnp.int32)]
    for grp in range(1, N_GROUPS):
        starts.append(starts[-1] + nch[grp - 1:grp, :])
    pos = jnp.zeros((1, tm), F32)
    for grp in range(N_GROUPS):
        pos = pos + ohf[grp:grp + 1, :] * (rank[grp:grp + 1, :] + (starts[grp] * CHUNK).astype(F32))
    pos_ref[0] = jnp.broadcast_to(pos, (META_ROWS, tm))
    meta_ref[0] = jnp.concatenate(
        [jnp.broadcast_to(starts[grp], (1, LANES)) for grp in range(N_GROUPS)]
        + [jnp.broadcast_to(nch[grp:grp + 1, :], (1, LANES)) for grp in range(N_GROUPS)], axis=0)

    p_mat = _one_hot_rows(pos, n_rows)
    xs_ref[0] = _nt_dot(x1.T.astype(BF16), p_mat).astype(BF16)

    kio = lax.broadcasted_iota(jnp.int32, (META_ROWS, tm), 0)
    gk = jnp.where(kio == k1, g1, 0.0) + jnp.where(kio == k2, g2, 0.0)
    p1 = gk.astype(BF16).astype(F32)
    r1 = gk - p1
    p2 = r1.astype(BF16).astype(F32)
    p3 = r1 - p2
    pieces = _nt_dot(jnp.concatenate([p1, p2, p3], axis=0).astype(BF16), p_mat)
    gs_ref[0] = pieces[0:META_ROWS] + pieces[META_ROWS:2 * META_ROWS] + pieces[2 * META_ROWS:]


def _dispatch(attn, x2d, wo_bf, ln_g, ln_b, wr_split, br_col, tri, *, tm):
    t = x2d.shape[0]
    nt = t // tm
    n_rows = _sorted_rows(tm)
    return pl.pallas_call(
        _dispatch_kernel,
        grid=(nt,),
        in_specs=[pl.BlockSpec((tm, D_MODEL), lambda i: (i, 0)),
                  pl.BlockSpec((tm, D_MODEL), lambda i: (i, 0)),
                  pl.BlockSpec((D_MODEL, D_MODEL), lambda i: (0, 0)),
                  pl.BlockSpec((1, D_MODEL), lambda i: (0, 0)),
                  pl.BlockSpec((1, D_MODEL), lambda i: (0, 0)),
                  pl.BlockSpec((2 * N_EXPERTS, D_MODEL), lambda i: (0, 0)),
                  pl.BlockSpec((N_EXPERTS, 1), lambda i: (0, 0)),
                  pl.BlockSpec((tm, tm), lambda i: (0, 0))],
        out_specs=[pl.BlockSpec((tm, D_MODEL), lambda i: (i, 0)),
                   pl.BlockSpec((1, D_MODEL, n_rows), lambda i: (i, 0, 0)),
                   pl.BlockSpec((1, META_ROWS, n_rows), lambda i: (i, 0, 0)),
                   pl.BlockSpec((1, META_ROWS, tm), lambda i: (i, 0, 0)),
                   pl.BlockSpec((1, META_ROWS, LANES), lambda i: (i, 0, 0))],
        out_shape=[jax.ShapeDtypeStruct((t, D_MODEL), F32),
                   jax.ShapeDtypeStruct((nt, D_MODEL, n_rows), BF16),
                   jax.ShapeDtypeStruct((nt, META_ROWS, n_rows), F32),
                   jax.ShapeDtypeStruct((nt, META_ROWS, tm), F32),
                   jax.ShapeDtypeStruct((nt, META_ROWS, LANES), jnp.int32)],
        compiler_params=_params(("parallel",)),
        name="dispatch",
    )(attn, x2d, wo_bf, ln_g, ln_b, wr_split, br_col, tri)


def _experts_kernel(meta_ref, xs_ref, gs_ref, wg_ref, wu_ref, wd_ref, ys_ref):
    i = pl.program_id(0)
    grp = pl.program_id(1)

    @pl.when(grp == 0)
    def _():
        ys_ref[...] = jnp.zeros_like(ys_ref)

    first = meta_ref[i * META_ROWS + grp]
    n_chunks = meta_ref[i * META_ROWS + N_GROUPS + grp]

    def run(c, width):
        col = pl.multiple_of((first + c) * CHUNK, CHUNK)
        xc = xs_ref[0, :, pl.ds(col, width)]
        gates = gs_ref[0, :, pl.ds(col, width)]
        acc = jnp.zeros((D_MODEL, width), F32)
        for k in range(EXPERTS_PER_GROUP):
            hg = jnp.dot(wg_ref[k], xc, preferred_element_type=F32)
            hu = jnp.dot(wu_ref[k], xc, preferred_element_type=F32)
            h = hg * jax.nn.sigmoid(hg) * hu
            y = jnp.dot(wd_ref[k], h.astype(BF16), preferred_element_type=F32)
            acc = acc + gates[k:k + 1, :] * y
        ys_ref[0, :, pl.ds(col, width)] = acc.astype(BF16)

    def chunk_pair(c2, carry):
        run(2 * c2, 2 * CHUNK)
        return carry

    lax.fori_loop(0, n_chunks >> 1, chunk_pair, 0)

    @pl.when((n_chunks & 1) == 1)
    def _():
        run(n_chunks - 1, CHUNK)


def _experts(meta, xs, gs, wg_t, wu_t, wd_t):
    nt, _, n_rows = xs.shape
    grid_spec = pltpu.PrefetchScalarGridSpec(
        num_scalar_prefetch=1,
        grid=(nt, N_GROUPS),
        in_specs=[pl.BlockSpec((1, D_MODEL, n_rows), lambda i, g, m: (i, 0, 0)),
                  pl.BlockSpec((1, META_ROWS, n_rows), lambda i, g, m: (i, 0, 0)),
                  pl.BlockSpec((EXPERTS_PER_GROUP, D_EXPERT, D_MODEL), lambda i, g, m: (g, 0, 0)),
                  pl.BlockSpec((EXPERTS_PER_GROUP, D_EXPERT, D_MODEL), lambda i, g, m: (g, 0, 0)),
                  pl.BlockSpec((EXPERTS_PER_GROUP, D_MODEL, D_EXPERT), lambda i, g, m: (g, 0, 0))],
        out_specs=pl.BlockSpec((1, D_MODEL, n_rows), lambda i, g, m: (i, 0, 0)),
    )
    return pl.pallas_call(
        _experts_kernel,
        grid_spec=grid_spec,
        out_shape=jax.ShapeDtypeStruct((nt, D_MODEL, n_rows), BF16),
        compiler_params=_params(("parallel", "arbitrary")),
        name="experts",
    )(meta, xs, gs, wg_t, wu_t, wd_t)


def _combine_kernel(ys_ref, pos_ref, x1_ref, g_ref, b_ref, p_ref, wple_ref, wpg_ref, out_ref):
    n_rows = ys_ref.shape[2]
    p_mat = _one_hot_rows(pos_ref[0, 0:1, :], n_rows)
    ffn_t = jnp.dot(ys_ref[0], p_mat, preferred_element_type=F32)
    x2 = _layer_norm(DEEPNORM_ALPHA * x1_ref[...] + ffn_t.T, g_ref[...], b_ref[...])
    ple = jnp.dot(p_ref[...].astype(BF16), wple_ref[...], preferred_element_type=F32)
    gt = jax.nn.sigmoid(jnp.dot(x2.astype(BF16), wpg_ref[...], preferred_element_type=F32))
    out_ref[...] = x2 + ple * gt


def _combine(ys, pos, x1, ln_g, ln_b, p2d, wple_bf, wpg_bf, *, tm):
    t = x1.shape[0]
    n_rows = ys.shape[2]
    return pl.pallas_call(
        _combine_kernel,
        grid=(t // tm,),
        in_specs=[pl.BlockSpec((1, D_MODEL, n_rows), lambda i: (i, 0, 0)),
                  pl.BlockSpec((1, META_ROWS, tm), lambda i: (i, 0, 0)),
                  pl.BlockSpec((tm, D_MODEL), lambda i: (i, 0)),
                  pl.BlockSpec((1, D_MODEL), lambda i: (0, 0)),
                  pl.BlockSpec((1, D_MODEL), lambda i: (0, 0)),
                  pl.BlockSpec((tm, PLE_DIM), lambda i: (i, 0)),
                  pl.BlockSpec((PLE_DIM, D_MODEL), lambda i: (0, 0)),
                  pl.BlockSpec((D_MODEL, D_MODEL), lambda i: (0, 0))],
        out_specs=pl.BlockSpec((tm, D_MODEL), lambda i: (i, 0)),
        out_shape=jax.ShapeDtypeStruct((t, D_MODEL), F32),
        compiler_params=_params(("parallel",)),
        name="combine",
    )(ys, pos, x1, ln_g, ln_b, p2d, wple_bf, wpg_bf)


def _routed_ffn_block(attn2d, x2d, p2d, layer, wts, *, tm):
    x1, xs, gs, pos, meta = _dispatch(attn2d, x2d, wts["w_o"][layer], wts["ln_g"][layer, 0:1],
                                      wts["ln_b"][layer, 0:1], wts["w_router"], wts["b_router"],
                                      wts["tri"], tm=tm)
    ys = _experts(meta[:, :, 0].reshape(-1), xs, gs, wts["w_gate_t"][layer], wts["w_up_t"][layer],
                  wts["w_down_t"][layer])
    return _combine(ys, pos, x1, wts["ln_g"][layer, 1:2], wts["ln_b"][layer, 1:2], p2d,
                    wts["w_ple"][layer], wts["w_ple_gate"][layer], tm=tm)


def _token_tile(t):
    return 512 if t % 512 == 0 else t


def _ffn_block(attn2d, x2d, p2d, layer, wts, *, tm):
    if x2d.shape[0] % MOE_TILE == 0:
        return _routed_ffn_block(attn2d, x2d, p2d, layer, wts, tm=MOE_TILE)
    x1, gates = _post_attn(attn2d, x2d, wts["w_o"][layer], wts["ln_g"][layer, 0:1], wts["ln_b"][layer, 0:1],
                           wts["w_router"], wts["b_router"], tm=tm)
    return _moe(x1, gates, wts["w_gate_t"][layer], wts["w_up_t"][layer], wts["w_down_t"][layer],
                wts["ln_g"][layer, 1:2], wts["ln_b"][layer, 1:2], p2d, wts["w_ple"][layer],
                wts["w_ple_gate"][layer], tm=tm)


def _prompt_trunk(x, p, wts):
    b, s, _ = x.shape
    t = b * s
    tm = _token_tile(s)
    cs = _rope_tables(jnp.arange(s))
    x2d = x.reshape(t, D_MODEL)

    q, k0, v0, khm, vt, km = _proj_rope(x2d, wts["w_qkv"][0], cs, seq=s, tm=tm, attn_layout=True, with_kmean=True)
    kmean = km.reshape(b, s // MOBA_BLOCK, KV_HEADS, HEAD_DIM).transpose(0, 2, 1, 3)
    attn = _moba_prompt(q.reshape(b, s, D_MODEL), khm, vt, kmean)
    x2d = _ffn_block(attn.reshape(t, D_MODEL), x2d, p[0].reshape(t, PLE_DIM), 0, wts, tm=tm)

    q, k1, v1, khm, vt = _proj_rope(x2d, wts["w_qkv"][1], cs, seq=s, tm=tm, attn_layout=True, with_kmean=False)
    attn = _swa_prompt(q.reshape(b, s, D_MODEL), khm, vt, wts["sinks"])
    x2d = _ffn_block(attn.reshape(t, D_MODEL), x2d, p[1].reshape(t, PLE_DIM), 1, wts, tm=tm)
    return (x2d.reshape(b, s, D_MODEL), _heads_last(k0), _heads_last(v0), _heads_last(k1), _heads_last(v1))


def _heads_last(kv_t):
    b, _, s = kv_t.shape
    return jnp.transpose(kv_t.reshape(b, KV_HEADS, HEAD_DIM, s), (0, 3, 1, 2))


def _sample_trunk(x, p, cache_k, cache_v, state_k, state_v, page_table, wts):
    db, tn, _ = x.shape
    t = db * tn
    tm = _token_tile(t)
    pos = PAST_LEN + jnp.arange(tn)
    cs = jnp.tile(_rope_tables(pos), (db, 1))
    x2d = x.reshape(t, D_MODEL)

    q, k0, v0 = _proj_rope(x2d, wts["w_qkv"][0], cs, seq=tn, tm=tm, attn_layout=False, with_kmean=False)
    o = _moba_sample(_block_diag_q(q.reshape(db, tn, D_MODEL)),
                     _pad_new(k0.reshape(db, tn, KV_DIM)), _pad_new(v0.reshape(db, tn, KV_DIM)),
                     _cache_pages_t(cache_k), _cache_pages_t(cache_v), page_table)
    attn = _block_diag_out(o, tn).astype(BF16)
    x2d = _ffn_block(attn.reshape(t, D_MODEL), x2d, p[0].reshape(t, PLE_DIM), 0, wts, tm=tm)

    q, k1, v1 = _proj_rope(x2d, wts["w_qkv"][1], cs, seq=tn, tm=tm, attn_layout=False, with_kmean=False)
    wb = state_k.shape[1]
    sink_rows = jnp.broadcast_to(
        jnp.broadcast_to(wts["sinks"].reshape(KV_HEADS, 1, GROUP), (KV_HEADS, tn, GROUP)).reshape(SAMPLE_ROWS, 1),
        (SAMPLE_ROWS, LANES))
    o = _swa_sample(_block_diag_q(q.reshape(db, tn, D_MODEL)),
                    _tokens_last(state_k), _tokens_last(state_v),
                    _pad_new(k1.reshape(db, tn, KV_DIM)), _pad_new(v1.reshape(db, tn, KV_DIM)), sink_rows)
    attn = _block_diag_out(o, tn).astype(BF16)
    x2d = _ffn_block(attn.reshape(t, D_MODEL), x2d, p[1].reshape(t, PLE_DIM), 1, wts, tm=tm)
    return (x2d.reshape(db, tn, D_MODEL), k0.reshape(db, tn, KV_HEADS, HEAD_DIM), v0.reshape(db, tn, KV_HEADS, HEAD_DIM),
            k1.reshape(db, tn, KV_HEADS, HEAD_DIM), v1.reshape(db, tn, KV_HEADS, HEAD_DIM))


def _split_hi_lo(w):
    hi = w.astype(BF16)
    lo = (w - hi.astype(F32)).astype(BF16)
    return jnp.concatenate([hi, lo], axis=0)


def _tokens_last(x):
    n, tokens = x.shape[0], x.shape[1]
    return jnp.transpose(x, (0, 2, 3, 1)).reshape(n, KV_DIM, tokens)


def _cache_pages_t(cache):
    return _tokens_last(cache.reshape(cache.shape[1:]))


def kernel(x_prompt, x_sample, p_prompt, p_sample, cache_k_a, cache_v_a, state_swa_k, state_swa_v, page_table,
           w_qkv_a, w_o_a, w_kv_s, w_q_b, w_o_b, sinks_b, ln_g, ln_b, w_router, b_router,
           w_exp_gate, w_exp_up, w_exp_down, w_ple, w_ple_gate):
    wts = {
        "w_qkv": jnp.stack([w_qkv_a[0], jnp.concatenate([w_q_b[0], w_kv_s], axis=1)]).astype(BF16),
        "w_o": jnp.stack([w_o_a[0], w_o_b[0]]).astype(BF16),
        "sinks": sinks_b[0].astype(F32),
        "ln_g": ln_g, "ln_b": ln_b,
        "w_router": _split_hi_lo(w_router.astype(F32).T),
        "b_router": b_router.astype(F32).reshape(N_EXPERTS, 1),
        "w_gate_t": jnp.swapaxes(w_exp_gate, 2, 3).astype(BF16),
        "w_up_t": jnp.swapaxes(w_exp_up, 2, 3).astype(BF16),
        "w_down_t": jnp.swapaxes(w_exp_down, 2, 3).astype(BF16),
        "tri": jnp.triu(jnp.ones((MOE_TILE, MOE_TILE), BF16), k=1),
        "w_ple": w_ple.astype(BF16), "w_ple_gate": w_ple_gate.astype(BF16),
    }
    y_p, ka_p, va_p, ks_p, vs_p = _prompt_trunk(x_prompt, p_prompt, wts)
    y_s, ka_s, va_s, ks_s, vs_s = _sample_trunk(x_sample, p_sample, cache_k_a, cache_v_a,
                                                state_swa_k, state_swa_v, page_table, wts)
    wb_p = min(WINDOW, x_prompt.shape[1])
    tn = x_sample.shape[1]
    return (y_p, y_s, ka_p[None], va_p[None], ka_s[None], va_s[None],
            ks_p[:, -wb_p:], vs_p[:, -wb_p:],
            jnp.concatenate([state_swa_k, ks_s], axis=1)[:, tn:],
            jnp.concatenate([state_swa_v, vs_s], axis=1)[:, tn:])
```

```python
import functools

import jax
import jax.numpy as jnp
from jax import lax
from jax.experimental import pallas as pl
from jax.experimental.pallas import tpu as pltpu

D_MODEL = 1024
HEAD_DIM = 64
HALF_DIM = HEAD_DIM // 2
N_HEADS = D_MODEL // HEAD_DIM
KV_HEADS = 4
GROUP = N_HEADS // KV_HEADS
KV_DIM = KV_HEADS * HEAD_DIM
MOBA_BLOCK = 256
BLOCK_SHIFT = MOBA_BLOCK.bit_length() - 1
MOBA_TOPK = 3
WINDOW = 128
PAGE_SIZE = 128
PAST_LEN = 16384
ROPE_THETA = 10000.0
N_EXPERTS = 16
N_GROUPS = 4
EXPERTS_PER_GROUP = N_EXPERTS // N_GROUPS
D_EXPERT = D_MODEL // 2
PLE_DIM = 256
DEPTH = 2
DEEPNORM_ALPHA = (2.0 * DEPTH) ** 0.25
LN_EPS = 1e-5
ATTN_SCALE = HEAD_DIM ** -0.5
LOG2E = 1.4426950408889634
Q_SCALE = ATTN_SCALE * LOG2E

LANES = 128
Q_ROWS = GROUP * MOBA_BLOCK
ROW_CHUNK = LANES
SAMPLE_ROWS = KV_HEADS * GROUP * 4
NEW_PAD = 8
NEW_ROWS = 128
PAGES_PER_ITEM = 32
CACHE_SLOTS = 4
DENOM_ROWS = 16
VMEM_LIMIT = 56 * 1024 * 1024

F32 = jnp.float32
BF16 = jnp.bfloat16
NEG_INF = float("-inf")
NT_DIMS = (((1,), (1,)), ((), ()))


def _nt_dot(a, b):
    return lax.dot_general(a, b, NT_DIMS, preferred_element_type=F32)


def _params(semantics):
    return pltpu.CompilerParams(dimension_semantics=semantics, vmem_limit_bytes=VMEM_LIMIT)


def _layer_norm(z, g, b):
    mu = jnp.mean(z, axis=-1, keepdims=True)
    zc = z - mu
    var = jnp.mean(zc * zc, axis=-1, keepdims=True)
    return zc * lax.rsqrt(var + LN_EPS) * g + b


def _rope_tables(pos):
    inv_freq = ROPE_THETA ** (-jnp.arange(HALF_DIM, dtype=F32) / HALF_DIM)
    ang = pos.astype(F32)[:, None] * inv_freq[None, :]
    cos = jnp.cos(ang)
    sin = jnp.sin(ang)
    cos_t = jnp.tile(cos, (1, 4))
    sin_t = jnp.tile(jnp.concatenate([-sin, sin], axis=1), (1, 2))
    return jnp.concatenate([cos_t, sin_t], axis=1)


def _proj_rope_kernel(x_ref, w_ref, cs_ref, q_ref, k_ref, v_ref, *rest, attn_layout, with_kmean):
    x = x_ref[...].astype(BF16)
    qkv = jnp.dot(x, w_ref[...], preferred_element_type=F32)
    tm = x.shape[0]
    cos = cs_ref[:, :LANES]
    sin = cs_ref[:, LANES:]
    lane = lax.broadcasted_iota(jnp.int32, (tm, LANES), 1)
    first_half = (lane & HALF_DIM) == 0

    def rope(blk):
        partner = jnp.where(first_half, pltpu.roll(blk, LANES - HALF_DIM, 1), pltpu.roll(blk, HALF_DIM, 1))
        return blk * cos + partner * sin

    for j in range(D_MODEL // LANES):
        q_ref[:, j * LANES:(j + 1) * LANES] = (rope(qkv[:, j * LANES:(j + 1) * LANES]) * Q_SCALE).astype(BF16)
    k = jnp.concatenate([rope(qkv[:, D_MODEL + j * LANES:D_MODEL + (j + 1) * LANES])
                         for j in range(KV_DIM // LANES)], axis=1)
    v = qkv[:, D_MODEL + KV_DIM:]
    if attn_layout:
        khm_ref, vt_ref = rest[0], rest[1]
        v_t = v.T
        k_ref[0] = k.T
        v_ref[0] = v_t
        for h in range(KV_HEADS):
            khm_ref[0, h] = k[:, h * HEAD_DIM:(h + 1) * HEAD_DIM].astype(BF16)
        vt_ref[0] = v_t.astype(BF16)
    else:
        k_ref[...] = k
        v_ref[...] = v
    if with_kmean:
        km_ref = rest[2]
        nb = tm // MOBA_BLOCK
        km_ref[0] = jnp.sum(k.reshape(nb, MOBA_BLOCK, KV_DIM), axis=1) * (1.0 / MOBA_BLOCK)


def _proj_rope(x2d, w_bf, cs, *, seq, tm, attn_layout, with_kmean):
    t = x2d.shape[0]
    nt = t // tm
    tab_tiles = cs.shape[0] // tm
    out_shape = [jax.ShapeDtypeStruct((t, D_MODEL), BF16)]
    out_specs = [pl.BlockSpec((tm, D_MODEL), lambda i: (i, 0))]
    if attn_layout:
        nseq = t // seq
        tiles_per_seq = seq // tm
        seq_tile = pl.BlockSpec((1, KV_DIM, tm), lambda i: (i // tiles_per_seq, 0, i % tiles_per_seq))
        out_shape += [jax.ShapeDtypeStruct((nseq, KV_DIM, seq), F32),
                      jax.ShapeDtypeStruct((nseq, KV_DIM, seq), F32),
                      jax.ShapeDtypeStruct((nseq, KV_HEADS, seq, HEAD_DIM), BF16),
                      jax.ShapeDtypeStruct((nseq, KV_DIM, seq), BF16)]
        out_specs += [seq_tile, seq_tile,
                      pl.BlockSpec((1, KV_HEADS, tm, HEAD_DIM),
                                   lambda i: (i // tiles_per_seq, 0, i % tiles_per_seq, 0)),
                      seq_tile]
    else:
        out_shape += [jax.ShapeDtypeStruct((t, KV_DIM), F32), jax.ShapeDtypeStruct((t, KV_DIM), F32)]
        out_specs += [pl.BlockSpec((tm, KV_DIM), lambda i: (i, 0)), pl.BlockSpec((tm, KV_DIM), lambda i: (i, 0))]
    if with_kmean:
        out_shape.append(jax.ShapeDtypeStruct((nt, tm // MOBA_BLOCK, KV_DIM), F32))
        out_specs.append(pl.BlockSpec((1, tm // MOBA_BLOCK, KV_DIM), lambda i: (i, 0, 0)))
    return pl.pallas_call(
        functools.partial(_proj_rope_kernel, attn_layout=attn_layout, with_kmean=with_kmean),
        grid=(nt,),
        in_specs=[pl.BlockSpec((tm, D_MODEL), lambda i: (i, 0)),
                  pl.BlockSpec(w_bf.shape, lambda i: (0, 0)),
                  pl.BlockSpec((tm, 2 * LANES), lambda i: (i % tab_tiles, 0))],
        out_specs=out_specs,
        out_shape=out_shape,
        compiler_params=_params(("parallel",)),
        name="proj_rope",
    )(x2d, w_bf, cs)


def _stack_heads(qb):
    return jnp.concatenate([qb[:, g * HEAD_DIM:(g + 1) * HEAD_DIM] for g in range(GROUP)], axis=0)


def _unstack_out(out_t):
    o2 = jnp.concatenate([out_t[:, g * MOBA_BLOCK:(g + 1) * MOBA_BLOCK] for g in range(GROUP)], axis=0)
    return o2.T.astype(BF16)


def _lane_chunks():
    return [slice(c * ROW_CHUNK, (c + 1) * ROW_CHUNK) for c in range(Q_ROWS // ROW_CHUNK)]


def _moba_prompt_kernel(q_ref, k_ref, vt_ref, km_ref, o_ref,
                        sel_ref, m_ref, a_ref, p_ref, acc_ref, s_ref, bm_ref):
    n = pl.program_id(2)
    nblk = km_ref.shape[2]
    qrows = _stack_heads(q_ref[0])

    gate = _nt_dot(km_ref[0, 0].astype(BF16), qrows)
    jio = lax.broadcasted_iota(jnp.int32, (nblk, Q_ROWS), 0)
    past = jio < n
    g = jnp.where(past, gate, NEG_INF)
    sel = jnp.zeros((nblk, Q_ROWS), jnp.bool_)
    for _ in range(MOBA_TOPK):
        mx = jnp.max(g, axis=0, keepdims=True)
        idx = jnp.min(jnp.where(g == mx, jio, nblk), axis=0, keepdims=True)
        hit = jio == idx
        sel = sel | hit
        g = jnp.where(hit, NEG_INF, g)
    sel_ref[...] = (sel & past).astype(F32)

    own = pl.multiple_of(n * MOBA_BLOCK, MOBA_BLOCK)
    k_own = k_ref[0, 0, pl.ds(own, MOBA_BLOCK), :]
    kio = lax.broadcasted_iota(jnp.int32, (MOBA_BLOCK, ROW_CHUNK), 0)
    lio = lax.broadcasted_iota(jnp.int32, (MOBA_BLOCK, ROW_CHUNK), 1)
    for ci, cs in enumerate(_lane_chunks()):
        s = _nt_dot(k_own, qrows[cs, :])
        qio = (lio + ci * ROW_CHUNK) & (MOBA_BLOCK - 1)
        s = jnp.where(kio <= qio, s, NEG_INF)
        m0 = jnp.max(s, axis=0, keepdims=True)
        p = jnp.exp2(s - m0)
        m_ref[:, cs] = m0
        p_ref[:, cs] = p.astype(BF16)
    acc_ref[...] = jnp.zeros_like(acc_ref)
    ones_rows = jnp.ones((DENOM_ROWS, MOBA_BLOCK), BF16)

    def pv(blk):
        off = pl.multiple_of(blk * MOBA_BLOCK, MOBA_BLOCK)
        v_ext = jnp.concatenate([vt_ref[0, :, pl.ds(off, MOBA_BLOCK)], ones_rows], axis=0)
        return jnp.dot(v_ext, p_ref[...], preferred_element_type=F32)

    def scores(j, slot):
        off = pl.multiple_of(j * MOBA_BLOCK, MOBA_BLOCK)
        k_j = k_ref[0, 0, pl.ds(off, MOBA_BLOCK), :]
        for ci, cs in enumerate(_lane_chunks()):
            sj = _nt_dot(k_j, qrows[cs, :])
            s_ref[slot, ci] = sj
            bm_ref[slot, :, cs] = jnp.max(sj, axis=0, keepdims=True)

    def softmax(j, slot):
        pv_prev = pv(jnp.where(j == 0, n, j - 1))
        sel_j = sel_ref[pl.ds(j, 1), :]
        for ci, cs in enumerate(_lane_chunks()):
            picked = sel_j[:, cs] > 0.0
            m_old = m_ref[:, cs]
            m_new = jnp.where(picked, jnp.maximum(m_old, bm_ref[slot, :, cs]), m_old)
            a = jnp.exp2(m_old - m_new)
            pj = jnp.exp2(s_ref[slot, ci] - jnp.where(picked, m_new, jnp.inf))
            m_ref[:, cs] = m_new
            a_ref[:, cs] = a
            p_ref[:, cs] = pj.astype(BF16)
        acc_ref[...] = a_ref[...] * (acc_ref[...] + pv_prev)

    scores(0, 0)

    def block_pair(i, carry):
        j = 2 * i
        scores(j + 1, 1)
        softmax(j, 0)
        scores(jnp.minimum(j + 2, n - 1), 0)
        softmax(j + 1, 1)
        return carry

    lax.fori_loop(0, n >> 1, block_pair, 0)

    @pl.when((n & 1) == 1)
    def _():
        softmax(n - 1, 0)

    last = jnp.where(n == 0, 0, n - 1)
    total = acc_ref[...] + pv(last)
    o_ref[0] = _unstack_out(total[:HEAD_DIM] / total[HEAD_DIM:HEAD_DIM + 1])


def _moba_prompt(q, khm, vt, kmean):
    b, s, _ = q.shape
    nblk = s // MOBA_BLOCK
    return pl.pallas_call(
        _moba_prompt_kernel,
        grid=(b, KV_HEADS, nblk),
        in_specs=[pl.BlockSpec((1, MOBA_BLOCK, KV_DIM), lambda bi, h, n: (bi, n, h)),
                  pl.BlockSpec((1, 1, s, HEAD_DIM), lambda bi, h, n: (bi, h, 0, 0)),
                  pl.BlockSpec((1, HEAD_DIM, s), lambda bi, h, n: (bi, h, 0)),
                  pl.BlockSpec((1, 1, nblk, HEAD_DIM), lambda bi, h, n: (bi, h, 0, 0))],
        out_specs=pl.BlockSpec((1, MOBA_BLOCK, KV_DIM), lambda bi, h, n: (bi, n, h)),
        out_shape=jax.ShapeDtypeStruct((b, s, D_MODEL), BF16),
        scratch_shapes=[pltpu.VMEM((nblk, Q_ROWS), F32),
                        pltpu.VMEM((1, Q_ROWS), F32),
                        pltpu.VMEM((1, Q_ROWS), F32),
                        pltpu.VMEM((MOBA_BLOCK, Q_ROWS), BF16),
                        pltpu.VMEM((HEAD_DIM + DENOM_ROWS, Q_ROWS), F32),
                        pltpu.VMEM((2, Q_ROWS // ROW_CHUNK, MOBA_BLOCK, ROW_CHUNK), F32),
                        pltpu.VMEM((2, 1, Q_ROWS), F32)],
        compiler_params=_params(("parallel", "parallel", "arbitrary")),
        name="moba_prompt",
    )(q, khm, vt, kmean)


SWA_KEYS = MOBA_BLOCK + WINDOW


def _swa_prompt_kernel(sink_ref, q_ref, k_ref, vt_ref, o_ref, d_ref, p_ref, bias_ref, s_ref, m_ref):
    h = pl.program_id(1)
    n = pl.program_id(2)
    blocks = q_ref.shape[1] // MOBA_BLOCK

    @pl.when(n == 0)
    def _():
        ki = lax.broadcasted_iota(jnp.int32, (SWA_KEYS, ROW_CHUNK), 0)
        lio = lax.broadcasted_iota(jnp.int32, (SWA_KEYS, ROW_CHUNK), 1)
        for ci, cs in enumerate(_lane_chunks()):
            rel = ((lio + ci * ROW_CHUNK) & (MOBA_BLOCK - 1)) - ki
            for variant, shift in ((0, 0), (1, WINDOW)):
                diff = rel + shift
                bias_ref[variant, ci] = jnp.where((diff >= 0) & (diff < WINDOW), 0.0, NEG_INF)

    for u in range(blocks):
        blk = n * blocks + u
        rows = slice(u * MOBA_BLOCK, (u + 1) * MOBA_BLOCK)
        qrows = _stack_heads(q_ref[0, rows, :])
        start = pl.multiple_of(jnp.maximum(blk * MOBA_BLOCK - WINDOW, 0), WINDOW)
        variant = jnp.minimum(blk, 1) if u == 0 else 1
        k_win = k_ref[0, 0, pl.ds(start, SWA_KEYS), :]
        for ci, cs in enumerate(_lane_chunks()):
            s = _nt_dot(k_win, qrows[cs, :]) + bias_ref[variant, ci]
            sink = sink_ref[h * GROUP + (ci * ROW_CHUNK) // MOBA_BLOCK] * LOG2E
            m = jnp.maximum(jnp.max(s, axis=0, keepdims=True), sink)
            s_ref[u, ci] = s
            m_ref[u, :, cs] = m
            d_ref[u, :, cs] = jnp.exp2(sink - m)
        for ci, cs in enumerate(_lane_chunks()):
            p_ref[u, :, cs] = jnp.exp2(s_ref[u, ci] - m_ref[u, :, cs]).astype(BF16)
        v_ext = jnp.concatenate([vt_ref[0, :, pl.ds(start, SWA_KEYS)], jnp.ones((DENOM_ROWS, SWA_KEYS), BF16)],
                                axis=0)
        total = jnp.dot(v_ext, p_ref[u], preferred_element_type=F32)
        o_ref[0, rows, :] = _unstack_out(total[:HEAD_DIM] / (total[HEAD_DIM:HEAD_DIM + 1] + d_ref[u]))


def _swa_prompt(q, khm, vt, sinks):
    b, s, _ = q.shape
    nblk = s // MOBA_BLOCK
    blocks = 2 if nblk % 2 == 0 else 1
    grid_spec = pltpu.PrefetchScalarGridSpec(
        num_scalar_prefetch=1,
        grid=(b, KV_HEADS, nblk // blocks),
        in_specs=[pl.BlockSpec((1, blocks * MOBA_BLOCK, KV_DIM), lambda bi, h, n, sk: (bi, n, h)),
                  pl.BlockSpec((1, 1, s, HEAD_DIM), lambda bi, h, n, sk: (bi, h, 0, 0)),
                  pl.BlockSpec((1, HEAD_DIM, s), lambda bi, h, n, sk: (bi, h, 0))],
        out_specs=pl.BlockSpec((1, blocks * MOBA_BLOCK, KV_DIM), lambda bi, h, n, sk: (bi, n, h)),
        scratch_shapes=[pltpu.VMEM((blocks, 1, Q_ROWS), F32),
                        pltpu.VMEM((blocks, SWA_KEYS, Q_ROWS), BF16),
                        pltpu.VMEM((2, Q_ROWS // ROW_CHUNK, SWA_KEYS, ROW_CHUNK), F32),
                        pltpu.VMEM((blocks, Q_ROWS // ROW_CHUNK, SWA_KEYS, ROW_CHUNK), F32),
                        pltpu.VMEM((blocks, 1, Q_ROWS), F32)],
    )
    return pl.pallas_call(
        _swa_prompt_kernel,
        grid_spec=grid_spec,
        out_shape=jax.ShapeDtypeStruct((b, s, D_MODEL), BF16),
        compiler_params=_params(("parallel", "parallel", "arbitrary")),
        name="swa_prompt",
    )(sinks, q, khm, vt)


def _block_diag_q(q):
    db, t, _ = q.shape
    q5 = q.reshape(db, t, KV_HEADS, GROUP, HEAD_DIM).transpose(0, 2, 1, 3, 4)
    q5 = q5.reshape(db, KV_HEADS, t * GROUP, HEAD_DIM)
    eye = jnp.eye(KV_HEADS, dtype=q.dtype)
    return jnp.einsum("ikrd,kc->ikrcd", q5, eye).reshape(db, KV_HEADS * t * GROUP, KV_DIM)


def _block_diag_out(o, t):
    db = o.shape[0]
    o5 = o.reshape(db, KV_HEADS, t * GROUP, KV_HEADS, HEAD_DIM)
    od = jnp.stack([o5[:, h, :, h, :] for h in range(KV_HEADS)], axis=1)
    return od.reshape(db, KV_HEADS, t, GROUP, HEAD_DIM).transpose(0, 2, 1, 3, 4).reshape(db, t, D_MODEL)


def _pad_new(x):
    return jnp.pad(x, ((0, 0), (0, NEW_PAD - x.shape[1]), (0, 0)))


def _row_token(shape):
    return (lax.broadcasted_iota(jnp.int32, shape, 0) >> 2) & 3


def _new_scores(qbd, knew):
    pad = jnp.zeros((NEW_ROWS - NEW_PAD, KV_DIM), BF16)
    s_new = _nt_dot(qbd, jnp.concatenate([knew.astype(BF16), pad], axis=0))
    tcol = lax.broadcasted_iota(jnp.int32, s_new.shape, 1)
    return s_new, tcol <= _row_token(s_new.shape)


def _new_token_part(e_new, v_new):
    out = e_new[:, 0:1] * v_new[0:1, :]
    for t in range(1, 4):
        out = out + e_new[:, t:t + 1] * v_new[t:t + 1, :]
    return out


def _moba_sample_kernel(pt_ref, qbd_ref, knew_ref, vnew_ref, ck_ref, cv_ref, o_ref,
                        buf, sem, s_all, p_all, gate_s, acc_s):
    i = pl.program_id(0)
    ns = pl.num_programs(0)
    n_pages = pt_ref.shape[1]
    k_items = n_pages // PAGES_PER_ITEM
    n_items = 2 * k_items
    item_cols = PAGES_PER_ITEM * PAGE_SIZE
    blocks_per_item = item_cols // MOBA_BLOCK
    n_blocks = k_items * blocks_per_item
    nb_pad = gate_s.shape[1]
    qbd = qbd_ref[0]

    def page_copy(src, page, slot, pg):
        return pltpu.make_async_copy(src.at[page], buf.at[slot, :, pl.ds(pg * PAGE_SIZE, PAGE_SIZE)],
                                     sem.at[slot])

    def start_item(sample, it, slot):
        src = ck_ref if it < k_items else cv_ref
        c = it % k_items
        for pg in range(PAGES_PER_ITEM):
            page_copy(src, pt_ref[sample, c * PAGES_PER_ITEM + pg], slot, pg).start()

    def wait_item(slot):
        for pg in range(PAGES_PER_ITEM):
            page_copy(ck_ref, 0, slot, pg).wait()

    ahead = CACHE_SLOTS - 1

    @pl.when(i == 0)
    def _():
        for it in range(ahead):
            start_item(0, it, it)

    gate_s[...] = jnp.zeros_like(gate_s)
    glane = lax.broadcasted_iota(jnp.int32, (SAMPLE_ROWS, nb_pad), 1)
    own_out = None
    l_tot = None
    for it in range(n_items):
        slot = it % CACHE_SLOTS
        nxt = it + ahead
        if nxt < n_items:
            start_item(i, nxt, nxt % CACHE_SLOTS)
        else:
            @pl.when(i + 1 < ns)
            def _(nxt=nxt):
                start_item(i + 1, nxt - n_items, nxt % CACHE_SLOTS)
        wait_item(slot)

        if it < k_items:
            s = jnp.dot(qbd, buf[slot].astype(BF16), preferred_element_type=F32)
            s_all[:, it * item_cols:(it + 1) * item_cols] = s
            gate = gate_s[...]
            for bi in range(blocks_per_item):
                gate = jnp.where(glane == it * blocks_per_item + bi,
                                 jnp.sum(s[:, bi * MOBA_BLOCK:(bi + 1) * MOBA_BLOCK], axis=1, keepdims=True), gate)
            gate_s[...] = gate

        if it == k_items - 1:
            g = jnp.where(glane < n_blocks, gate_s[...], NEG_INF)
            sel = jnp.zeros((SAMPLE_ROWS, nb_pad), jnp.bool_)
            for _ in range(min(MOBA_TOPK, n_blocks)):
                mx = jnp.max(g, axis=1, keepdims=True)
                idx = jnp.min(jnp.where(g == mx, glane, nb_pad), axis=1, keepdims=True)
                hit = glane == idx
                sel = sel | hit
                g = jnp.where(hit, NEG_INF, g)
            sel_bf = sel.astype(BF16)

            s_new, new_ok = _new_scores(qbd, knew_ref[0])
            m = jnp.max(jnp.where(new_ok, s_new, NEG_INF), axis=1, keepdims=True)

            def expand(c):
                blk = lax.broadcasted_iota(jnp.int32, (nb_pad, item_cols), 0)
                key_blk = lax.broadcasted_iota(jnp.int32, (nb_pad, item_cols), 1) >> BLOCK_SHIFT
                ex = (blk == key_blk + c * blocks_per_item).astype(BF16)
                return jnp.dot(sel_bf, ex, preferred_element_type=F32) > 0.5

            for c in range(k_items):
                sc = jnp.where(expand(c), s_all[:, c * item_cols:(c + 1) * item_cols], NEG_INF)
                s_all[:, c * item_cols:(c + 1) * item_cols] = sc
                m = jnp.maximum(m, jnp.max(sc, axis=1, keepdims=True))
            e_new = jnp.where(new_ok, jnp.exp2(s_new - m), 0.0)
            l_tot = jnp.sum(e_new, axis=1, keepdims=True)
            for c in range(k_items):
                pc = jnp.exp2(s_all[:, c * item_cols:(c + 1) * item_cols] - m)
                l_tot = l_tot + jnp.sum(pc, axis=1, keepdims=True)
                p_all[:, c * item_cols:(c + 1) * item_cols] = pc.astype(BF16)
            own_out = _new_token_part(e_new, vnew_ref[0])
            acc_s[...] = jnp.zeros_like(acc_s)

        if it >= k_items:
            c = it - k_items
            acc_s[...] += _nt_dot(p_all[:, c * item_cols:(c + 1) * item_cols], buf[slot].astype(BF16))

    o_ref[0] = (acc_s[...] + own_out) / l_tot


def _moba_sample(qbd, k_new, v_new, cache_kt, cache_vt, page_table):
    db = qbd.shape[0]
    n_pages = page_table.shape[1]
    past = n_pages * PAGE_SIZE
    item_cols = PAGES_PER_ITEM * PAGE_SIZE
    nb_pad = -(-(past // MOBA_BLOCK) // LANES) * LANES
    assert n_pages % PAGES_PER_ITEM == 0 and (2 * n_pages // PAGES_PER_ITEM) % CACHE_SLOTS == 0
    grid_spec = pltpu.PrefetchScalarGridSpec(
        num_scalar_prefetch=1,
        grid=(db,),
        in_specs=[pl.BlockSpec((1, SAMPLE_ROWS, KV_DIM), lambda i, pt: (i, 0, 0)),
                  pl.BlockSpec((1, NEW_PAD, KV_DIM), lambda i, pt: (i, 0, 0)),
                  pl.BlockSpec((1, NEW_PAD, KV_DIM), lambda i, pt: (i, 0, 0)),
                  pl.BlockSpec(memory_space=pl.ANY),
                  pl.BlockSpec(memory_space=pl.ANY)],
        out_specs=pl.BlockSpec((1, SAMPLE_ROWS, KV_DIM), lambda i, pt: (i, 0, 0)),
        scratch_shapes=[pltpu.VMEM((CACHE_SLOTS, KV_DIM, item_cols), F32),
                        pltpu.SemaphoreType.DMA((CACHE_SLOTS,)),
                        pltpu.VMEM((SAMPLE_ROWS, past), F32),
                        pltpu.VMEM((SAMPLE_ROWS, past), BF16),
                        pltpu.VMEM((SAMPLE_ROWS, nb_pad), F32),
                        pltpu.VMEM((SAMPLE_ROWS, KV_DIM), F32)],
    )
    return pl.pallas_call(
        _moba_sample_kernel,
        grid_spec=grid_spec,
        out_shape=jax.ShapeDtypeStruct((db, SAMPLE_ROWS, KV_DIM), F32),
        compiler_params=_params(("arbitrary",)),
        name="moba_sample",
    )(page_table, qbd, k_new, v_new, cache_kt, cache_vt)


def _swa_sample_kernel(qbd_ref, kbuf_ref, vbuf_ref, knew_ref, vnew_ref, sink_ref, o_ref):
    wb = kbuf_ref.shape[2]
    bcol = lax.broadcasted_iota(jnp.int32, (SAMPLE_ROWS, wb), 1)
    bdiff = wb + _row_token((SAMPLE_ROWS, wb)) - bcol
    buf_ok = (bdiff >= 0) & (bdiff < WINDOW)
    sink = sink_ref[:, 0:1] * LOG2E
    for b in range(qbd_ref.shape[0]):
        qbd = qbd_ref[b]
        s_buf = jnp.dot(qbd, kbuf_ref[b].astype(BF16), preferred_element_type=F32)
        s_new, new_ok = _new_scores(qbd, knew_ref[b])
        m = jnp.maximum(jnp.max(jnp.where(buf_ok, s_buf, NEG_INF), axis=1, keepdims=True),
                        jnp.max(jnp.where(new_ok, s_new, NEG_INF), axis=1, keepdims=True))
        m = jnp.maximum(m, sink)
        e_buf = jnp.where(buf_ok, jnp.exp2(s_buf - m), 0.0)
        e_new = jnp.where(new_ok, jnp.exp2(s_new - m), 0.0)
        denom = (jnp.sum(e_buf, axis=1, keepdims=True) + jnp.sum(e_new, axis=1, keepdims=True)
                 + jnp.exp2(sink - m))
        out = _nt_dot(e_buf.astype(BF16), vbuf_ref[b].astype(BF16))
        o_ref[b] = (out + _new_token_part(e_new, vnew_ref[b])) / denom


def _swa_sample(qbd, k_buf_t, v_buf_t, k_new, v_new, sink_rows):
    db, _, wb = k_buf_t.shape
    sb = next(c for c in (8, 4, 2, 1) if db % c == 0)
    return pl.pallas_call(
        _swa_sample_kernel,
        grid=(db // sb,),
        in_specs=[pl.BlockSpec((sb, SAMPLE_ROWS, KV_DIM), lambda i: (i, 0, 0)),
                  pl.BlockSpec((sb, KV_DIM, wb), lambda i: (i, 0, 0)),
                  pl.BlockSpec((sb, KV_DIM, wb), lambda i: (i, 0, 0)),
                  pl.BlockSpec((sb, NEW_PAD, KV_DIM), lambda i: (i, 0, 0)),
                  pl.BlockSpec((sb, NEW_PAD, KV_DIM), lambda i: (i, 0, 0)),
                  pl.BlockSpec((SAMPLE_ROWS, LANES), lambda i: (0, 0))],
        out_specs=pl.BlockSpec((sb, SAMPLE_ROWS, KV_DIM), lambda i: (i, 0, 0)),
        out_shape=jax.ShapeDtypeStruct((db, SAMPLE_ROWS, KV_DIM), F32),
        compiler_params=_params(("parallel",)),
        name="swa_sample",
    )(qbd, k_buf_t, v_buf_t, k_new, v_new, sink_rows)


def _first_of(vals, m):
    idx = jnp.full(m.shape, EXPERTS_PER_GROUP - 1, jnp.int32)
    for k in range(EXPERTS_PER_GROUP - 2, -1, -1):
        idx = jnp.where(vals[k] == m, k, idx)
    return idx


def _route_picks(aff, bias):
    biased = aff + bias
    best = e1 = e2 = None
    for grp in range(N_GROUPS):
        v = [biased[grp * EXPERTS_PER_GROUP + k:grp * EXPERTS_PER_GROUP + k + 1, :] for k in range(EXPERTS_PER_GROUP)]
        m1 = jnp.maximum(jnp.maximum(v[0], v[1]), jnp.maximum(v[2], v[3]))
        i1 = _first_of(v, m1)
        v2 = [jnp.where(i1 == k, NEG_INF, v[k]) for k in range(EXPERTS_PER_GROUP)]
        m2 = jnp.maximum(jnp.maximum(v2[0], v2[1]), jnp.maximum(v2[2], v2[3]))
        i2 = _first_of(v2, m2)
        gs = m1 + m2
        if grp == 0:
            best, e1, e2 = gs, i1, i2
        else:
            better = gs > best
            best = jnp.where(better, gs, best)
            e1 = jnp.where(better, grp * EXPERTS_PER_GROUP + i1, e1)
            e2 = jnp.where(better, grp * EXPERTS_PER_GROUP + i2, e2)
    eio = lax.broadcasted_iota(jnp.int32, aff.shape, 0)
    sel1 = eio == e1
    sel2 = eio == e2
    a1 = jnp.sum(jnp.where(sel1, aff, 0.0), axis=0, keepdims=True)
    a2 = jnp.sum(jnp.where(sel2, aff, 0.0), axis=0, keepdims=True)
    den = a1 + a2
    return e1, e2, a1 / den, a2 / den


def _route_t(aff, bias):
    e1, e2, g1, g2 = _route_picks(aff, bias)
    eio = lax.broadcasted_iota(jnp.int32, aff.shape, 0)
    return jnp.where(eio == e1, g1, 0.0) + jnp.where(eio == e2, g2, 0.0)


def _router_affinity(x1, wr_ref):
    x_hi = x1.astype(BF16)
    x_lo = (x1 - x_hi.astype(F32)).astype(BF16)
    w2 = wr_ref[...]
    t_hi = _nt_dot(w2, x_hi)
    t_lo = _nt_dot(w2[:N_EXPERTS], x_lo)
    return jax.nn.sigmoid(t_hi[:N_EXPERTS] + t_hi[N_EXPERTS:] + t_lo)


def _post_attn_kernel(a_ref, x_ref, wo_ref, g_ref, b_ref, wr_ref, br_ref, x1_ref, gates_ref):
    o = jnp.dot(a_ref[...], wo_ref[...], preferred_element_type=F32)
    x1 = _layer_norm(DEEPNORM_ALPHA * x_ref[...] + o, g_ref[...], b_ref[...])
    x1_ref[...] = x1
    gates_t = _route_t(_router_affinity(x1, wr_ref), br_ref[...])
    tm = x1.shape[0]
    gates_ref[...] = jnp.concatenate([gates_t, jnp.zeros((LANES - N_EXPERTS, tm), F32)], axis=0).T


def _post_attn(attn, x2d, wo_bf, ln_g, ln_b, wr_split, br_col, *, tm):
    t = x2d.shape[0]
    return pl.pallas_call(
        _post_attn_kernel,
        grid=(t // tm,),
        in_specs=[pl.BlockSpec((tm, D_MODEL), lambda i: (i, 0)),
                  pl.BlockSpec((tm, D_MODEL), lambda i: (i, 0)),
                  pl.BlockSpec((D_MODEL, D_MODEL), lambda i: (0, 0)),
                  pl.BlockSpec((1, D_MODEL), lambda i: (0, 0)),
                  pl.BlockSpec((1, D_MODEL), lambda i: (0, 0)),
                  pl.BlockSpec((2 * N_EXPERTS, D_MODEL), lambda i: (0, 0)),
                  pl.BlockSpec((N_EXPERTS, 1), lambda i: (0, 0))],
        out_specs=[pl.BlockSpec((tm, D_MODEL), lambda i: (i, 0)),
                   pl.BlockSpec((tm, LANES), lambda i: (i, 0))],
        out_shape=[jax.ShapeDtypeStruct((t, D_MODEL), F32),
                   jax.ShapeDtypeStruct((t, LANES), F32)],
        compiler_params=_params(("parallel",)),
        name="post_attn",
    )(attn, x2d, wo_bf, ln_g, ln_b, wr_split, br_col)


def _moe_kernel(x1_ref, gates_ref, wg_ref, wu_ref, wd_ref, g_ref, b_ref, p_ref, wple_ref, wpg_ref,
                out_ref, xb_ref, acc_ref):
    e = pl.program_id(1)

    @pl.when(e == 0)
    def _():
        xb_ref[...] = x1_ref[...].astype(BF16)
        acc_ref[...] = jnp.zeros_like(acc_ref)

    xb = xb_ref[...]
    hg = _nt_dot(xb, wg_ref[0])
    hu = _nt_dot(xb, wu_ref[0])
    h = hg * jax.nn.sigmoid(hg) * hu
    y = _nt_dot(h.astype(BF16), wd_ref[0])
    gates = gates_ref[...]
    lane = lax.broadcasted_iota(jnp.int32, gates.shape, 1)
    gate_e = jnp.sum(jnp.where(lane == e, gates, 0.0), axis=1, keepdims=True)
    acc_ref[...] += gate_e * y

    @pl.when(e == pl.num_programs(1) - 1)
    def _():
        x2 = _layer_norm(DEEPNORM_ALPHA * x1_ref[...] + acc_ref[...], g_ref[...], b_ref[...])
        ple = jnp.dot(p_ref[...].astype(BF16), wple_ref[...], preferred_element_type=F32)
        gt = jax.nn.sigmoid(jnp.dot(x2.astype(BF16), wpg_ref[...], preferred_element_type=F32))
        out_ref[...] = x2 + ple * gt


def _moe(x1, gates, wg_bf, wu_bf, wd_bf, ln_g, ln_b, p2d, wple_bf, wpg_bf, *, tm):
    t = x1.shape[0]
    return pl.pallas_call(
        _moe_kernel,
        grid=(t // tm, N_EXPERTS),
        in_specs=[pl.BlockSpec((tm, D_MODEL), lambda i, e: (i, 0)),
                  pl.BlockSpec((tm, LANES), lambda i, e: (i, 0)),
                  pl.BlockSpec((1, D_EXPERT, D_MODEL), lambda i, e: (e, 0, 0)),
                  pl.BlockSpec((1, D_EXPERT, D_MODEL), lambda i, e: (e, 0, 0)),
                  pl.BlockSpec((1, D_MODEL, D_EXPERT), lambda i, e: (e, 0, 0)),
                  pl.BlockSpec((1, D_MODEL), lambda i, e: (0, 0)),
                  pl.BlockSpec((1, D_MODEL), lambda i, e: (0, 0)),
                  pl.BlockSpec((tm, PLE_DIM), lambda i, e: (i, 0)),
                  pl.BlockSpec((PLE_DIM, D_MODEL), lambda i, e: (0, 0)),
                  pl.BlockSpec((D_MODEL, D_MODEL), lambda i, e: (0, 0))],
        out_specs=pl.BlockSpec((tm, D_MODEL), lambda i, e: (i, 0)),
        out_shape=jax.ShapeDtypeStruct((t, D_MODEL), F32),
        scratch_shapes=[pltpu.VMEM((tm, D_MODEL), BF16),
                        pltpu.VMEM((tm, D_MODEL), F32)],
        compiler_params=_params(("parallel", "arbitrary")),
        name="moe",
    )(x1, gates, wg_bf, wu_bf, wd_bf, ln_g, ln_b, p2d, wple_bf, wpg_bf)


MOE_TILE = 1024
CHUNK = LANES
META_ROWS = 8


def _sorted_rows(tm):
    return tm + N_GROUPS * CHUNK


def _one_hot_rows(pos_row, n_rows):
    tm = pos_row.shape[1]
    jio = lax.broadcasted_iota(jnp.int32, (n_rows, tm), 0)
    return (jio == pos_row.astype(jnp.int32)).astype(BF16)


def _dispatch_kernel(a_ref, x_ref, wo_ref, g_ref, b_ref, wr_ref, br_ref, tri_ref,
                     x1_ref, xs_ref, gs_ref, pos_ref, meta_ref):
    o = jnp.dot(a_ref[...], wo_ref[...], preferred_element_type=F32)
    x1 = _layer_norm(DEEPNORM_ALPHA * x_ref[...] + o, g_ref[...], b_ref[...])
    x1_ref[...] = x1
    tm = x1.shape[0]
    n_rows = xs_ref.shape[2]
    e1, e2, g1, g2 = _route_picks(_router_affinity(x1, wr_ref), br_ref[...])
    gid = e1 >> 2
    k1 = e1 & (EXPERTS_PER_GROUP - 1)
    k2 = e2 & (EXPERTS_PER_GROUP - 1)

    gio = lax.broadcasted_iota(jnp.int32, (META_ROWS, tm), 0)
    oh = gio == gid
    ohf = oh.astype(F32)
    rank = jnp.dot(oh.astype(BF16), tri_ref[...], preferred_element_type=F32)
    count = jnp.sum(ohf, axis=1, keepdims=True).astype(jnp.int32)
    nch = (count + (CHUNK - 1)) >> (CHUNK.bit_length() - 1)
    starts = [jnp.zeros((1, 1), jnp.int32)]
    for grp in range(1, N_GROUPS):
        starts.append(starts[-1] + nch[grp - 1:grp, :])
    pos = jnp.zeros((1, tm), F32)
    for grp in range(N_GROUPS):
        pos = pos + ohf[grp:grp + 1, :] * (rank[grp:grp + 1, :] + (starts[grp] * CHUNK).astype(F32))
    pos_ref[0] = jnp.broadcast_to(pos, (META_ROWS, tm))
    meta_ref[0] = jnp.concatenate(
        [jnp.broadcast_to(starts[grp], (1, LANES)) for grp in range(N_GROUPS)]
        + [jnp.broadcast_to(nch[grp:grp + 1, :], (1, LANES)) for grp in range(N_GROUPS)], axis=0)

    p_mat = _one_hot_rows(pos, n_rows)
    xs_ref[0] = _nt_dot(x1.T.astype(BF16), p_mat).astype(BF16)

    kio = lax.broadcasted_iota(jnp.int32, (META_ROWS, tm), 0)
    gk = jnp.where(kio == k1, g1, 0.0) + jnp.where(kio == k2, g2, 0.0)
    p1 = gk.astype(BF16).astype(F32)
    r1 = gk - p1
    p2 = r1.astype(BF16).astype(F32)
    p3 = r1 - p2
    pieces = _nt_dot(jnp.concatenate([p1, p2, p3], axis=0).astype(BF16), p_mat)
    gs_ref[0] = pieces[0:META_ROWS] + pieces[META_ROWS:2 * META_ROWS] + pieces[2 * META_ROWS:]


def _dispatch(attn, x2d, wo_bf, ln_g, ln_b, wr_split, br_col, tri, *, tm):
    t = x2d.shape[0]
    nt = t // tm
    n_rows = _sorted_rows(tm)
    return pl.pallas_call(
        _dispatch_kernel,
        grid=(nt,),
        in_specs=[pl.BlockSpec((tm, D_MODEL), lambda i: (i, 0)),
                  pl.BlockSpec((tm, D_MODEL), lambda i: (i, 0)),
                  pl.BlockSpec((D_MODEL, D_MODEL), lambda i: (0, 0)),
                  pl.BlockSpec((1, D_MODEL), lambda i: (0, 0)),
                  pl.BlockSpec((1, D_MODEL), lambda i: (0, 0)),
                  pl.BlockSpec((2 * N_EXPERTS, D_MODEL), lambda i: (0, 0)),
                  pl.BlockSpec((N_EXPERTS, 1), lambda i: (0, 0)),
                  pl.BlockSpec((tm, tm), lambda i: (0, 0))],
        out_specs=[pl.BlockSpec((tm, D_MODEL), lambda i: (i, 0)),
                   pl.BlockSpec((1, D_MODEL, n_rows), lambda i: (i, 0, 0)),
                   pl.BlockSpec((1, META_ROWS, n_rows), lambda i: (i, 0, 0)),
                   pl.BlockSpec((1, META_ROWS, tm), lambda i: (i, 0, 0)),
                   pl.BlockSpec((1, META_ROWS, LANES), lambda i: (i, 0, 0))],
        out_shape=[jax.ShapeDtypeStruct((t, D_MODEL), F32),
                   jax.ShapeDtypeStruct((nt, D_MODEL, n_rows), BF16),
                   jax.ShapeDtypeStruct((nt, META_ROWS, n_rows), F32),
                   jax.ShapeDtypeStruct((nt, META_ROWS, tm), F32),
                   jax.ShapeDtypeStruct((nt, META_ROWS, LANES), jnp.int32)],
        compiler_params=_params(("parallel",)),
        name="dispatch",
    )(attn, x2d, wo_bf, ln_g, ln_b, wr_split, br_col, tri)


def _experts_kernel(meta_ref, xs_ref, gs_ref, wg_ref, wu_ref, wd_ref, ys_ref):
    i = pl.program_id(0)
    grp = pl.program_id(1)

    @pl.when(grp == 0)
    def _():
        ys_ref[...] = jnp.zeros_like(ys_ref)

    first = meta_ref[i * META_ROWS + grp]
    n_chunks = meta_ref[i * META_ROWS + N_GROUPS + grp]

    def run(c, width):
        col = pl.multiple_of((first + c) * CHUNK, CHUNK)
        xc = xs_ref[0, :, pl.ds(col, width)]
        gates = gs_ref[0, :, pl.ds(col, width)]
        acc = jnp.zeros((D_MODEL, width), F32)
        for k in range(EXPERTS_PER_GROUP):
            hg = jnp.dot(wg_ref[k], xc, preferred_element_type=F32)
            hu = jnp.dot(wu_ref[k], xc, preferred_element_type=F32)
            h = hg * jax.nn.sigmoid(hg) * hu
            y = jnp.dot(wd_ref[k], h.astype(BF16), preferred_element_type=F32)
            acc = acc + gates[k:k + 1, :] * y
        ys_ref[0, :, pl.ds(col, width)] = acc.astype(BF16)

    def chunk_pair(c2, carry):
        run(2 * c2, 2 * CHUNK)
        return carry

    lax.fori_loop(0, n_chunks >> 1, chunk_pair, 0)

    @pl.when((n_chunks & 1) == 1)
    def _():
        run(n_chunks - 1, CHUNK)


def _experts(meta, xs, gs, wg_t, wu_t, wd_t):
    nt, _, n_rows = xs.shape
    grid_spec = pltpu.PrefetchScalarGridSpec(
        num_scalar_prefetch=1,
        grid=(nt, N_GROUPS),
        in_specs=[pl.BlockSpec((1, D_MODEL, n_rows), lambda i, g, m: (i, 0, 0)),
                  pl.BlockSpec((1, META_ROWS, n_rows), lambda i, g, m: (i, 0, 0)),
                  pl.BlockSpec((EXPERTS_PER_GROUP, D_EXPERT, D_MODEL), lambda i, g, m: (g, 0, 0)),
                  pl.BlockSpec((EXPERTS_PER_GROUP, D_EXPERT, D_MODEL), lambda i, g, m: (g, 0, 0)),
                  pl.BlockSpec((EXPERTS_PER_GROUP, D_MODEL, D_EXPERT), lambda i, g, m: (g, 0, 0))],
        out_specs=pl.BlockSpec((1, D_MODEL, n_rows), lambda i, g, m: (i, 0, 0)),
    )
    return pl.pallas_call(
        _experts_kernel,
        grid_spec=grid_spec,
        out_shape=jax.ShapeDtypeStruct((nt, D_MODEL, n_rows), BF16),
        compiler_params=_params(("parallel", "arbitrary")),
        name="experts",
    )(meta, xs, gs, wg_t, wu_t, wd_t)


def _combine_kernel(ys_ref, pos_ref, x1_ref, g_ref, b_ref, p_ref, wple_ref, wpg_ref, out_ref):
    n_rows = ys_ref.shape[2]
    p_mat = _one_hot_rows(pos_ref[0, 0:1, :], n_rows)
    ffn_t = jnp.dot(ys_ref[0], p_mat, preferred_element_type=F32)
    x2 = _layer_norm(DEEPNORM_ALPHA * x1_ref[...] + ffn_t.T, g_ref[...], b_ref[...])
    ple = jnp.dot(p_ref[...].astype(BF16), wple_ref[...], preferred_element_type=F32)
    gt = jax.nn.sigmoid(jnp.dot(x2.astype(BF16), wpg_ref[...], preferred_element_type=F32))
    out_ref[...] = x2 + ple * gt


def _combine(ys, pos, x1, ln_g, ln_b, p2d, wple_bf, wpg_bf, *, tm):
    t = x1.shape[0]
    n_rows = ys.shape[2]
    return pl.pallas_call(
        _combine_kernel,
        grid=(t // tm,),
        in_specs=[pl.BlockSpec((1, D_MODEL, n_rows), lambda i: (i, 0, 0)),
                  pl.BlockSpec((1, META_ROWS, tm), lambda i: (i, 0, 0)),
                  pl.BlockSpec((tm, D_MODEL), lambda i: (i, 0)),
                  pl.BlockSpec((1, D_MODEL), lambda i: (0, 0)),
                  pl.BlockSpec((1, D_MODEL), lambda i: (0, 0)),
                  pl.BlockSpec((tm, PLE_DIM), lambda i: (i, 0)),
                  pl.BlockSpec((PLE_DIM, D_MODEL), lambda i: (0, 0)),
                  pl.BlockSpec((D_MODEL, D_MODEL), lambda i: (0, 0))],
        out_specs=pl.BlockSpec((tm, D_MODEL), lambda i: (i, 0)),
        out_shape=jax.ShapeDtypeStruct((t, D_MODEL), F32),
        compiler_params=_params(("parallel",)),
        name="combine",
    )(ys, pos, x1, ln_g, ln_b, p2d, wple_bf, wpg_bf)


def _routed_ffn_block(attn2d, x2d, p2d, layer, wts, *, tm):
    x1, xs, gs, pos, meta = _dispatch(attn2d, x2d, wts["w_o"][layer], wts["ln_g"][layer, 0:1],
                                      wts["ln_b"][layer, 0:1], wts["w_router"], wts["b_router"],
                                      wts["tri"], tm=tm)
    ys = _experts(meta[:, :, 0].reshape(-1), xs, gs, wts["w_gate_t"][layer], wts["w_up_t"][layer],
                  wts["w_down_t"][layer])
    return _combine(ys, pos, x1, wts["ln_g"][layer, 1:2], wts["ln_b"][layer, 1:2], p2d,
                    wts["w_ple"][layer], wts["w_ple_gate"][layer], tm=tm)


def _token_tile(t):
    return 512 if t % 512 == 0 else t


def _ffn_block(attn2d, x2d, p2d, layer, wts, *, tm):
    if x2d.shape[0] % MOE_TILE == 0:
        return _routed_ffn_block(attn2d, x2d, p2d, layer, wts, tm=MOE_TILE)
    x1, gates = _post_attn(attn2d, x2d, wts["w_o"][layer], wts["ln_g"][layer, 0:1], wts["ln_b"][layer, 0:1],
                           wts["w_router"], wts["b_router"], tm=tm)
    return _moe(x1, gates, wts["w_gate_t"][layer], wts["w_up_t"][layer], wts["w_down_t"][layer],
                wts["ln_g"][layer, 1:2], wts["ln_b"][layer, 1:2], p2d, wts["w_ple"][layer],
                wts["w_ple_gate"][layer], tm=tm)


def _prompt_trunk(x, p, wts):
    b, s, _ = x.shape
    t = b * s
    tm = _token_tile(s)
    cs = _rope_tables(jnp.arange(s))
    x2d = x.reshape(t, D_MODEL)

    q, k0, v0, khm, vt, km = _proj_rope(x2d, wts["w_qkv"][0], cs, seq=s, tm=tm, attn_layout=True, with_kmean=True)
    kmean = km.reshape(b, s // MOBA_BLOCK, KV_HEADS, HEAD_DIM).transpose(0, 2, 1, 3)
    attn = _moba_prompt(q.reshape(b, s, D_MODEL), khm, vt, kmean)
    x2d = _ffn_block(attn.reshape(t, D_MODEL), x2d, p[0].reshape(t, PLE_DIM), 0, wts, tm=tm)

    q, k1, v1, khm, vt = _proj_rope(x2d, wts["w_qkv"][1], cs, seq=s, tm=tm, attn_layout=True, with_kmean=False)
    attn = _swa_prompt(q.reshape(b, s, D_MODEL), khm, vt, wts["sinks"])
    x2d = _ffn_block(attn.reshape(t, D_MODEL), x2d, p[1].reshape(t, PLE_DIM), 1, wts, tm=tm)
    return (x2d.reshape(b, s, D_MODEL), _heads_last(k0), _heads_last(v0), _heads_last(k1), _heads_last(v1))


def _heads_last(kv_t):
    b, _, s = kv_t.shape
    return jnp.transpose(kv_t.reshape(b, KV_HEADS, HEAD_DIM, s), (0, 3, 1, 2))


def _sample_trunk(x, p, cache_k, cache_v, state_k, state_v, page_table, wts):
    db, tn, _ = x.shape
    t = db * tn
    tm = _token_tile(t)
    pos = PAST_LEN + jnp.arange(tn)
    cs = jnp.tile(_rope_tables(pos), (db, 1))
    x2d = x.reshape(t, D_MODEL)

    q, k0, v0 = _proj_rope(x2d, wts["w_qkv"][0], cs, seq=tn, tm=tm, attn_layout=False, with_kmean=False)
    o = _moba_sample(_block_diag_q(q.reshape(db, tn, D_MODEL)),
                     _pad_new(k0.reshape(db, tn, KV_DIM)), _pad_new(v0.reshape(db, tn, KV_DIM)),
                     _cache_pages_t(cache_k), _cache_pages_t(cache_v), page_table)
    attn = _block_diag_out(o, tn).astype(BF16)
    x2d = _ffn_block(attn.reshape(t, D_MODEL), x2d, p[0].reshape(t, PLE_DIM), 0, wts, tm=tm)

    q, k1, v1 = _proj_rope(x2d, wts["w_qkv"][1], cs, seq=tn, tm=tm, attn_layout=False, with_kmean=False)
    wb = state_k.shape[1]
    sink_rows = jnp.broadcast_to(
        jnp.broadcast_to(wts["sinks"].reshape(KV_HEADS, 1, GROUP), (KV_HEADS, tn, GROUP)).reshape(SAMPLE_ROWS, 1),
        (SAMPLE_ROWS, LANES))
    o = _swa_sample(_block_diag_q(q.reshape(db, tn, D_MODEL)),
                    _tokens_last(state_k), _tokens_last(state_v),
                    _pad_new(k1.reshape(db, tn, KV_DIM)), _pad_new(v1.reshape(db, tn, KV_DIM)), sink_rows)
    attn = _block_diag_out(o, tn).astype(BF16)
    x2d = _ffn_block(attn.reshape(t, D_MODEL), x2d, p[1].reshape(t, PLE_DIM), 1, wts, tm=tm)
    return (x2d.reshape(db, tn, D_MODEL), k0.reshape(db, tn, KV_HEADS, HEAD_DIM), v0.reshape(db, tn, KV_HEADS, HEAD_DIM),
            k1.reshape(db, tn, KV_HEADS, HEAD_DIM), v1.reshape(db, tn, KV_HEADS, HEAD_DIM))


def _split_hi_lo(w):
    hi = w.astype(BF16)
    lo = (w - hi.astype(F32)).astype(BF16)
    return jnp.concatenate([hi, lo], axis=0)


def _tokens_last(x):
    n, tokens = x.shape[0], x.shape[1]
    return jnp.transpose(x, (0, 2, 3, 1)).reshape(n, KV_DIM, tokens)


def _cache_pages_t(cache):
    return _tokens_last(cache.reshape(cache.shape[1:]))


def kernel(x_prompt, x_sample, p_prompt, p_sample, cache_k_a, cache_v_a, state_swa_k, state_swa_v, page_table,
           w_qkv_a, w_o_a, w_kv_s, w_q_b, w_o_b, sinks_b, ln_g, ln_b, w_router, b_router,
           w_exp_gate, w_exp_up, w_exp_down, w_ple, w_ple_gate):
    wts = {
        "w_qkv": jnp.stack([w_qkv_a[0], jnp.concatenate([w_q_b[0], w_kv_s], axis=1)]).astype(BF16),
        "w_o": jnp.stack([w_o_a[0], w_o_b[0]]).astype(BF16),
        "sinks": sinks_b[0].astype(F32),
        "ln_g": ln_g, "ln_b": ln_b,
        "w_router": _split_hi_lo(w_router.astype(F32).T),
        "b_router": b_router.astype(F32).reshape(N_EXPERTS, 1),
        "w_gate_t": jnp.swapaxes(w_exp_gate, 2, 3).astype(BF16),
        "w_up_t": jnp.swapaxes(w_exp_up, 2, 3).astype(BF16),
        "w_down_t": jnp.swapaxes(w_exp_down, 2, 3).astype(BF16),
        "tri": jnp.triu(jnp.ones((MOE_TILE, MOE_TILE), BF16), k=1),
        "w_ple": w_ple.astype(BF16), "w_ple_gate": w_ple_gate.astype(BF16),
    }
    y_p, ka_p, va_p, ks_p, vs_p = _prompt_trunk(x_prompt, p_prompt, wts)
    y_s, ka_s, va_s, ks_s, vs_s = _sample_trunk(x_sample, p_sample, cache_k_a, cache_v_a,
                                                state_swa_k, state_swa_v, page_table, wts)
    wb_p = min(WINDOW, x_prompt.shape[1])
    tn = x_sample.shape[1]
    return (y_p, y_s, ka_p[None], va_p[None], ka_s[None], va_s[None],
            ks_p[:, -wb_p:], vs_p[:, -wb_p:],
            jnp.concatenate([state_swa_k, ks_s], axis=1)[:, tn:],
            jnp.concatenate([state_swa_v, vs_s], axis=1)[:, tn:])
```
